```python
import math
import jax, jax.numpy as jnp
from jax import lax
import numpy as np

D_MODEL = 1024
BATCH = 2
SEQ = 16384
DEPTH = 1
DEC_BATCH = 4
DEC_SEQ = 8192
PAST_LEN = 128

EPS = 1e-6
MLA_HEADS = 4
QK_NOPE_DIM = 128
QK_ROPE_DIM = 64
V_HEAD_DIM = 128
Q_LORA_RANK = 384
KV_LORA_RANK = 256
ROPE_THETA = 10000.0
Q_BLOCK = 128
GDN_HEADS = 4
GDN_DK = 128
GDN_DV = 128
CONV_WIDTH = 5
CHUNK = 64
D_FF = 4 * D_MODEL
N_MOD = 6

MLA_WIDTH = MLA_HEADS * V_HEAD_DIM
GDN_WIDTH = GDN_HEADS * GDN_DV
MIX_WIDTH = MLA_WIDTH + GDN_WIDTH
GDN_CONV_CH = 2 * GDN_HEADS * GDN_DK + GDN_HEADS * GDN_DV
IN_SPLITS = (Q_LORA_RANK, KV_LORA_RANK, QK_ROPE_DIM, GDN_CONV_CH, GDN_WIDTH,
             GDN_HEADS, GDN_HEADS, GDN_HEADS, GDN_HEADS)
IN_COLS = sum(IN_SPLITS)

kernel_name = "hymba_mla_gdn_adaln_encoder"


def _split_cols(t, sizes):
    idx = np.cumsum(np.array(sizes))[:-1].tolist()
    return jnp.split(t, idx, axis=-1)


def rms_norm(x, g):
    xf = x.astype(jnp.float32)
    y = xf * lax.rsqrt(jnp.mean(xf * xf, axis=-1, keepdims=True) + EPS)
    return (y * g.astype(jnp.float32)).astype(x.dtype)


def l2_norm(x):
    xf = x.astype(jnp.float32)
    return xf * lax.rsqrt(jnp.sum(xf * xf, axis=-1, keepdims=True) + EPS)


def rotary_tables(S):
    inv = 1.0 / (ROPE_THETA ** (jnp.arange(0, QK_ROPE_DIM, 2, dtype=jnp.float32) / QK_ROPE_DIM))
    ang = jnp.arange(S, dtype=jnp.float32)[:, None] * inv[None, :]
    return jnp.cos(ang), jnp.sin(ang)


def apply_rope(x, cos, sin):
    xf = x.astype(jnp.float32)
    x1, x2 = jnp.split(xf, 2, axis=-1)
    return jnp.concatenate([x1 * cos - x2 * sin, x2 * cos + x1 * sin], axis=-1).astype(x.dtype)


def mla_mixer(cq, ckv, kr, g_q, w_uq, g_kv, w_ukv):
    B, S, _ = cq.shape
    cos, sin = rotary_tables(S)
    scale = (QK_NOPE_DIM + QK_ROPE_DIM) ** -0.5
    q = (rms_norm(cq, g_q) @ w_uq).reshape(B, S, MLA_HEADS, QK_NOPE_DIM + QK_ROPE_DIM)
    q_nope = q[..., :QK_NOPE_DIM] * scale
    q_rope = apply_rope(q[..., QK_NOPE_DIM:], cos[:, None, :], sin[:, None, :]) * scale
    kv = (rms_norm(ckv, g_kv) @ w_ukv).reshape(B, S, MLA_HEADS, QK_NOPE_DIM + V_HEAD_DIM)
    k_nope, v = kv[..., :QK_NOPE_DIM], kv[..., QK_NOPE_DIM:]
    k_rope = apply_rope(kr, cos, sin)
    nb = S // Q_BLOCK

    def blocks(t):
        return jnp.moveaxis(t.reshape(B, nb, Q_BLOCK, *t.shape[2:]), 1, 0)

    def attend(qb):
        qn, qr = qb
        s = (jnp.einsum('bqhd,bkhd->bhqk', qn, k_nope)
             + jnp.einsum('bqhr,bkr->bhqk', qr, k_rope)).astype(jnp.float32)
        p = jax.nn.softmax(s, axis=-1).astype(v.dtype)
        return jnp.einsum('bhqk,bkhd->bqhd', p, v)

    o = lax.map(attend, (blocks(q_nope), blocks(q_rope)))
    return jnp.moveaxis(o, 0, 1).reshape(B, S, MLA_WIDTH)


def gated_delta_chunked(q, k, v, g, beta):
    B, S, H, Dk = q.shape
    Dv = v.shape[-1]
    C = CHUNK
    N = S // C
    f32 = jnp.float32
    q = q.astype(f32) * (Dk ** -0.5)
    k = k.astype(f32)
    v = v.astype(f32)

    def to_chunks(t):
        return t.reshape(B, N, C, H, t.shape[-1]).transpose(0, 3, 1, 2, 4)

    q, k, v = to_chunks(q), to_chunks(k), to_chunks(v)
    g = g.astype(f32).reshape(B, N, C, H).transpose(0, 3, 1, 2)
    beta = beta.astype(f32).reshape(B, N, C, H).transpose(0, 3, 1, 2)
    g = jnp.cumsum(g, axis=-1)
    k_beta = k * beta[..., None]
    v_beta = v * beta[..., None]
    lower = jnp.tril(jnp.ones((C, C), dtype=bool))
    strict = jnp.tril(jnp.ones((C, C), dtype=bool), -1)
    diff = g[..., :, None] - g[..., None, :]
    decay = jnp.where(lower, jnp.exp(jnp.where(lower, diff, 0.0)), 0.0)
    a_mat = jnp.where(strict, jnp.einsum('bhncd,bhnmd->bhncm', k_beta, k) * decay, 0.0)
    t_mat = a_mat + jnp.eye(C, dtype=f32)
    u = lax.linalg.triangular_solve(t_mat, v_beta, left_side=True, lower=True, unit_diagonal=True)
    w = lax.linalg.triangular_solve(t_mat, k_beta * jnp.exp(g)[..., None], left_side=True,
                                    lower=True, unit_diagonal=True)
    attn = jnp.where(lower, jnp.einsum('bhncd,bhnmd->bhncm', q, k) * decay, 0.0)
    g_last = g[..., -1]
    k_tail = k * jnp.exp(g_last[..., None] - g)[..., None]
    q_dec = q * jnp.exp(g)[..., None]

    def step(state, inp):
        q_i, kt_i, u_i, w_i, at_i, gl_i = inp
        v_new = u_i - jnp.einsum('bhcd,bhde->bhce', w_i, state)
        o_i = jnp.einsum('bhcd,bhde->bhce', q_i, state) + jnp.einsum('bhcm,bhme->bhce', at_i, v_new)
        state = state * jnp.exp(gl_i)[..., None, None] + jnp.einsum('bhcd,bhce->bhde', kt_i, v_new)
        return state, o_i

    xs = tuple(jnp.moveaxis(t, 2, 0) for t in (q_dec, k_tail, u, w, attn, g_last))
    s0 = jnp.zeros((B, H, Dk, Dv), f32)
    _, o = lax.scan(step, s0, xs)
    return o.transpose(1, 0, 3, 2, 4).reshape(B, S, H, Dv)


def gdn_mixer(qkv, z, a_f, a_b, b_f, b_b, conv_w, a_log_f, a_log_b, dt_f, dt_b, g_gdn):
    B, S, _ = qkv.shape
    pad = CONV_WIDTH // 2
    qkv = lax.conv_general_dilated(qkv, conv_w[:, None, :].astype(qkv.dtype), window_strides=(1,),
                                   padding=[(pad, pad)], dimension_numbers=('NWC', 'WIO', 'NWC'),
                                   feature_group_count=GDN_CONV_CH)
    qkv = jax.nn.silu(qkv)
    q, k, v = _split_cols(qkv, (GDN_HEADS * GDN_DK, GDN_HEADS * GDN_DK, GDN_WIDTH))
    q = l2_norm(q.reshape(B, S, GDN_HEADS, GDN_DK))
    k = l2_norm(k.reshape(B, S, GDN_HEADS, GDN_DK))
    v = v.reshape(B, S, GDN_HEADS, GDN_DV)

    def log_decay(a, a_log, dt):
        return -jnp.exp(a_log.astype(jnp.float32)) * jax.nn.softplus(a.astype(jnp.float32) + dt.astype(jnp.float32))

    g_fwd, g_bwd = log_decay(a_f, a_log_f, dt_f), log_decay(a_b, a_log_b, dt_b)
    beta_fwd, beta_bwd = jax.nn.sigmoid(b_f.astype(jnp.float32)), jax.nn.sigmoid(b_b.astype(jnp.float32))
    flip = lambda t: jnp.flip(t, axis=1)
    o_fwd = gated_delta_chunked(q, k, v, g_fwd, beta_fwd)
    o_bwd = flip(gated_delta_chunked(flip(q), flip(k), flip(v), flip(g_bwd), flip(beta_bwd)))
    o = rms_norm(o_fwd + o_bwd, g_gdn) * jax.nn.silu(z.reshape(B, S, GDN_HEADS, GDN_DV).astype(jnp.float32))
    return o.reshape(B, S, GDN_WIDTH).astype(z.dtype)


def encoder_layer(x, mod, g_mix, w_in, g_q, w_uq, g_kv, w_ukv, conv_w, a_log_f, a_log_b, dt_f, dt_b,
                  g_gdn, w_out, g_mlp, w_mlp_in, w_mlp_out):
    shift_a, scale_a, gate_a, shift_m, scale_m, gate_m = jnp.split(mod[:, None, :], N_MOD, axis=-1)
    h = rms_norm(x, g_mix) * (1.0 + scale_a) + shift_a
    proj = h @ w_in
    cq, ckv, kr, qkv, z, a_f, a_b, b_f, b_b = _split_cols(proj, IN_SPLITS)
    o_mla = mla_mixer(cq, ckv, kr, g_q, w_uq, g_kv, w_ukv)
    o_gdn = gdn_mixer(qkv, z, a_f, a_b, b_f, b_b, conv_w, a_log_f, a_log_b, dt_f, dt_b, g_gdn)
    mixed = jnp.concatenate([o_mla, o_gdn], axis=-1) @ w_out
    x = x + gate_a * mixed
    h = rms_norm(x, g_mlp) * (1.0 + scale_m) + shift_m
    x = x + gate_m * (jnp.square(jax.nn.relu(h @ w_mlp_in)) @ w_mlp_out)
    return x


def run_trunk(x, c, w_ada, b_ada, g_mix, w_in, g_q, w_uq, g_kv, w_ukv, conv_w, a_log_f, a_log_b,
              dt_f, dt_b, g_gdn, w_out, g_mlp, w_mlp_in, w_mlp_out, w_ada_f, b_ada_f, g_final):
    sc = jax.nn.silu(c)
    for l in range(DEPTH):
        mod = sc @ w_ada[l] + b_ada[l]
        x = encoder_layer(x, mod, g_mix[l], w_in[l], g_q[l], w_uq[l], g_kv[l], w_ukv[l], conv_w[l],
                          a_log_f[l], a_log_b[l], dt_f[l], dt_b[l], g_gdn[l], w_out[l], g_mlp[l],
                          w_mlp_in[l], w_mlp_out[l])
    shift_f, scale_f = jnp.split((sc @ w_ada_f + b_ada_f)[:, None, :], 2, axis=-1)
    return rms_norm(x, g_final) * (1.0 + scale_f) + shift_f


def setup_inputs(seed: int = 0) -> dict:
    key = jax.random.key(seed)
    ks = jax.random.split(key, 26)
    f32 = jnp.float32
    L, D = DEPTH, D_MODEL

    def nrm(k, shape, fan_in):
        return jax.random.normal(k, shape, f32) * (fan_in ** -0.5)

    def gain(k, shape):
        return 1.0 + 0.02 * jax.random.normal(k, shape, f32)

    def dt_bias(k):
        dt = jnp.exp(jax.random.uniform(k, (L, GDN_HEADS), f32) * (math.log(0.1) - math.log(0.001)) + math.log(0.001))
        return dt + jnp.log(-jnp.expm1(-dt))

    return {
        "x_prompt": jax.random.normal(ks[0], (BATCH, SEQ, D), f32),
        "x_sample": jax.random.normal(ks[1], (DEC_BATCH, DEC_SEQ, D), f32),
        "c_prompt": jax.random.normal(ks[2], (BATCH, D), f32),
        "c_sample": jax.random.normal(ks[3], (DEC_BATCH, D), f32),
        "w_ada": nrm(ks[4], (L, D, N_MOD * D), D),
        "b_ada": 0.02 * jax.random.normal(ks[5], (L, N_MOD * D), f32),
        "g_mix": gain(ks[6], (L, D)),
        "w_in": nrm(ks[7], (L, D, IN_COLS), D),
        "g_q": gain(ks[8], (L, Q_LORA_RANK)),
        "w_uq": nrm(ks[9], (L, Q_LORA_RANK, MLA_HEADS * (QK_NOPE_DIM + QK_ROPE_DIM)), Q_LORA_RANK),
        "g_kv": gain(ks[10], (L, KV_LORA_RANK)),
        "w_ukv": nrm(ks[11], (L, KV_LORA_RANK, MLA_HEADS * (QK_NOPE_DIM + V_HEAD_DIM)), KV_LORA_RANK),
        "conv_w": nrm(ks[12], (L, CONV_WIDTH, GDN_CONV_CH), CONV_WIDTH),
        "a_log_f": jnp.log(jax.random.uniform(ks[13], (L, GDN_HEADS), f32, 1.0, 16.0)),
        "a_log_b": jnp.log(jax.random.uniform(ks[14], (L, GDN_HEADS), f32, 1.0, 16.0)),
        "dt_f": dt_bias(ks[15]),
        "dt_b": dt_bias(ks[16]),
        "g_gdn": gain(ks[17], (L, GDN_DV)),
        "w_out": nrm(ks[18], (L, MIX_WIDTH, D), MIX_WIDTH),
        "g_mlp": gain(ks[19], (L, D)),
        "w_mlp_in": nrm(ks[20], (L, D, D_FF), D),
        "w_mlp_out": nrm(ks[21], (L, D_FF, D), D_FF),
        "w_ada_f": nrm(ks[22], (D, 2 * D), D),
        "b_ada_f": 0.02 * jax.random.normal(ks[23], (2 * D,), f32),
        "g_final": gain(ks[24], (D,)),
    }


def reference(x_prompt, x_sample, c_prompt, c_sample, w_ada, b_ada, g_mix, w_in, g_q, w_uq, g_kv, w_ukv,
              conv_w, a_log_f, a_log_b, dt_f, dt_b, g_gdn, w_out, g_mlp, w_mlp_in, w_mlp_out,
              w_ada_f, b_ada_f, g_final):
    y_prompt = run_trunk(x_prompt, c_prompt, w_ada, b_ada, g_mix, w_in, g_q, w_uq, g_kv, w_ukv, conv_w,
                         a_log_f, a_log_b, dt_f, dt_b, g_gdn, w_out, g_mlp, w_mlp_in, w_mlp_out,
                         w_ada_f, b_ada_f, g_final)
    y_sample = run_trunk(x_sample, c_sample, w_ada, b_ada, g_mix, w_in, g_q, w_uq, g_kv, w_ukv, conv_w,
                         a_log_f, a_log_b, dt_f, dt_b, g_gdn, w_out, g_mlp, w_mlp_in, w_mlp_out,
                         w_ada_f, b_ada_f, g_final)
    return (y_prompt, y_sample)
```

```python
import functools

import jax
import jax.numpy as jnp
from jax import lax
from jax.experimental import pallas as pl
from jax.experimental.pallas import tpu as pltpu

F32 = jnp.float32
BF16 = jnp.bfloat16

EPS = 1e-6
D_MODEL = 1024
N_MOD = 6
MLA_HEADS = 4
QK_NOPE = 128
QK_ROPE = 64
V_DIM = 128
Q_LORA = 384
KV_LORA = 256
ROPE_THETA = 10000.0
GDN_HEADS = 4
GDN_DK = 128
GDN_DV = 128
CONV_W = 5
D_FF = 4 * D_MODEL
QK_PAD = 256
GDN_CH = 2 * GDN_HEADS * GDN_DK + GDN_HEADS * GDN_DV
GDN_W = GDN_HEADS * GDN_DV
MLA_W = MLA_HEADS * V_DIM
GATE_PAD = 128
C_CQ = 0
C_CKV = C_CQ + Q_LORA
C_KR = C_CKV + KV_LORA
C_QKV = C_KR + 128
C_Z = C_QKV + GDN_CH
C_GATE = C_Z + GDN_W
WA_COLS = C_GATE + GATE_PAD

CHUNK = 128
INV_BLK = 16
VMEM_LIMIT = 56 * 1024 * 1024


def _dot(a, b):
    return jnp.dot(a, b, preferred_element_type=F32)


def _dot_bf(a, b):
    return jnp.dot(a.astype(BF16), b.astype(BF16), preferred_element_type=F32)


def _split2(a):
    hi = a.astype(BF16)
    lo = (a - hi.astype(F32)).astype(BF16)
    return hi, lo


def _dot_x3(a, b):
    ah, al = _split2(a)
    bh, bl = _split2(b)
    return _dot(ah, bh) + (_dot(ah, bl) + _dot(al, bh))


def _sigmoid(x):
    return 1.0 / (1.0 + jnp.exp(-x))


def _silu(x):
    return x * _sigmoid(x)


def _cparams(sem):
    return pltpu.CompilerParams(dimension_semantics=sem, vmem_limit_bytes=VMEM_LIMIT)


def _ada_kernel(c_ref, w_ref, b_ref, o_ref):
    sc = _silu(c_ref[...])
    o_ref[...] = _dot_bf(sc, w_ref[...]) + b_ref[...]


def _ada(c8, w, b):
    d, n = w.shape
    tn = 1024
    return pl.pallas_call(
        _ada_kernel,
        out_shape=jax.ShapeDtypeStruct((8, n), F32),
        grid=(n // tn,),
        in_specs=[pl.BlockSpec((8, d), lambda j: (0, 0)),
                  pl.BlockSpec((d, tn), lambda j: (0, j)),
                  pl.BlockSpec((1, tn), lambda j: (0, j))],
        out_specs=pl.BlockSpec((8, tn), lambda j: (0, j)),
        compiler_params=_cparams(("parallel",)),
        name="ada",
    )(c8, w, b)


def _rms(x, g):
    ms = jnp.mean(x * x, axis=-1, keepdims=True)
    return x * lax.rsqrt(ms + EPS) * g


def _pre_kernel(x_ref, mod_ref, gmix_ref, wa_ref, gq_ref, wuq_ref, gkv_ref, wukv_ref, cs_ref, sn_ref,
                q_ref, k_ref, v_ref, qkv_ref, z_ref, gate_ref):
    d = D_MODEL
    x = x_ref[0]
    mod = mod_ref[0]
    shift = mod[:, 0:d]
    scale = mod[:, d:2 * d]
    hb = (_rms(x, gmix_ref[...]) * (1.0 + scale) + shift).astype(BF16)

    def proj(lo, hi):
        return _dot(hb, wa_ref[:, lo:hi])

    cs = cs_ref[...]
    sn = sn_ref[...]

    def rope(t):
        return t * cs + pltpu.roll(t, 64, 1) * sn

    qk_scale = (QK_NOPE + QK_ROPE) ** -0.5
    cq = _rms(proj(C_CQ, C_CKV), gq_ref[...])
    q = _dot_bf(cq, wuq_ref[...])
    ckv = _rms(proj(C_CKV, C_KR), gkv_ref[...])
    kv = _dot_bf(ckv, wukv_ref[...])
    kr = rope(proj(C_KR, C_QKV)).astype(BF16)
    for h in range(MLA_HEADS):
        o = h * QK_PAD
        q_ref[0, h, :, 0:128] = (q[:, o:o + 128] * qk_scale).astype(BF16)
        q_ref[0, h, :, 128:256] = (rope(q[:, o + 128:o + 256]) * qk_scale).astype(BF16)
        k_ref[0, h, :, 0:128] = kv[:, o:o + 128].astype(BF16)
        k_ref[0, h, :, 128:256] = kr
        v_ref[0, h] = kv[:, o + 128:o + 256].astype(BF16)
    qkv_ref[0] = proj(C_QKV, C_Z)
    z_ref[0] = proj(C_Z, C_GATE)
    gate_ref[0] = proj(C_GATE, WA_COLS)


def _pre(x, mod, gmix, wa, gq, wuq, gkv, wukv, cs, sn, ts):
    b, s, d = x.shape
    h = MLA_HEADS
    const = lambda bi, si: (0, 0)
    return pl.pallas_call(
        _pre_kernel,
        out_shape=(jax.ShapeDtypeStruct((b, h, s, QK_PAD), BF16),
                   jax.ShapeDtypeStruct((b, h, s, QK_PAD), BF16),
                   jax.ShapeDtypeStruct((b, h, s, V_DIM), BF16),
                   jax.ShapeDtypeStruct((b, s, GDN_CH), F32),
                   jax.ShapeDtypeStruct((b, s, GDN_W), F32),
                   jax.ShapeDtypeStruct((b, s, GATE_PAD), F32)),
        grid=(b, s // ts),
        in_specs=[pl.BlockSpec((1, ts, d), lambda bi, si: (bi, si, 0)),
                  pl.BlockSpec((1, 1, N_MOD * d), lambda bi, si: (bi, 0, 0)),
                  pl.BlockSpec((1, d), const),
                  pl.BlockSpec((d, WA_COLS), const),
                  pl.BlockSpec((1, Q_LORA), const),
                  pl.BlockSpec((Q_LORA, h * QK_PAD), const),
                  pl.BlockSpec((1, KV_LORA), const),
                  pl.BlockSpec((KV_LORA, h * 256), const),
                  pl.BlockSpec((ts, 128), lambda bi, si: (si, 0)),
                  pl.BlockSpec((ts, 128), lambda bi, si: (si, 0))],
        out_specs=(pl.BlockSpec((1, h, ts, QK_PAD), lambda bi, si: (bi, 0, si, 0)),
                   pl.BlockSpec((1, h, ts, QK_PAD), lambda bi, si: (bi, 0, si, 0)),
                   pl.BlockSpec((1, h, ts, V_DIM), lambda bi, si: (bi, 0, si, 0)),
                   pl.BlockSpec((1, ts, GDN_CH), lambda bi, si: (bi, si, 0)),
                   pl.BlockSpec((1, ts, GDN_W), lambda bi, si: (bi, si, 0)),
                   pl.BlockSpec((1, ts, GATE_PAD), lambda bi, si: (bi, si, 0))),
        compiler_params=_cparams(("parallel", "parallel")),
        name="pre",
    )(x, mod, gmix, wa, gq, wuq, gkv, wukv, cs, sn)


def _attn_kernel(q_ref, k_ref, v_ref, o_ref, *, tk, nk):
    q = q_ref[0, 0]
    tq = q.shape[0]

    def body(i, carry):
        m, l, acc = carry
        start = pl.multiple_of(i * tk, tk)
        k = k_ref[0, 0, pl.ds(start, tk), :]
        v = v_ref[0, 0, pl.ds(start, tk), :]
        s = lax.dot_general(q, k, (((1,), (1,)), ((), ())), preferred_element_type=F32)
        m_new = jnp.maximum(m, jnp.max(s, axis=-1, keepdims=True))
        alpha = jnp.exp(m - m_new)
        p = jnp.exp(s - m_new)
        l = alpha * l + jnp.sum(p, axis=-1, keepdims=True)
        acc = alpha * acc + _dot(p.astype(BF16), v)
        return m_new, l, acc

    m0 = jnp.full((tq, 1), -jnp.inf, F32)
    l0 = jnp.zeros((tq, 1), F32)
    a0 = jnp.zeros((tq, V_DIM), F32)
    _, l, acc = lax.fori_loop(0, nk, body, (m0, l0, a0))
    o_ref[0] = (acc / l).astype(o_ref.dtype)


def _attention(q, k, v, tq, tk):
    b, h, s, _ = q.shape
    return pl.pallas_call(
        functools.partial(_attn_kernel, tk=tk, nk=s // tk),
        out_shape=jax.ShapeDtypeStruct((b, s, h * V_DIM), BF16),
        grid=(b, h, s // tq),
        in_specs=[pl.BlockSpec((1, 1, tq, QK_PAD), lambda bi, hi, qi: (bi, hi, qi, 0)),
                  pl.BlockSpec((1, 1, s, QK_PAD), lambda bi, hi, qi: (bi, hi, 0, 0)),
                  pl.BlockSpec((1, 1, s, V_DIM), lambda bi, hi, qi: (bi, hi, 0, 0))],
        out_specs=pl.BlockSpec((1, tq, V_DIM), lambda bi, hi, qi: (bi, qi, hi)),
        compiler_params=_cparams(("parallel", "parallel", "parallel")),
        name="attn",
    )(q, k, v)


def _tri_inverse(a, row, col):
    eye = (row == col).astype(F32)
    blk = (row // INV_BLK) == (col // INV_BLK)
    dg = jnp.where(blk, a, 0.0)
    off = a - dg
    d2 = _dot_x3(dg, dg)
    d4 = _dot_x3(d2, d2)
    d8 = _dot_x3(d4, d4)
    p = eye - dg
    p = p + _dot_x3(p, d2)
    p = p + _dot_x3(p, d4)
    p = p + _dot_x3(p, d8)
    m = _dot_x3(p, off)
    m2 = _dot_x3(m, m)
    m4 = _dot_x3(m2, m2)
    r = eye - m
    r = r + _dot_x3(r, m2)
    r = r + _dot_x3(r, m4)
    return _dot_x3(r, p)


def _tri_sum(tri_bf, g, tri_on_left):
    g1 = g.astype(BF16)
    r1 = g - g1.astype(F32)
    g2 = r1.astype(BF16)
    g3 = (r1 - g2.astype(F32)).astype(BF16)
    if tri_on_left:
        return _dot(tri_bf, g1) + (_dot(tri_bf, g2) + _dot(tri_bf, g3))
    return _dot(g1, tri_bf) + (_dot(g2, tri_bf) + _dot(g3, tri_bf))


def _gdn_kernel(prev_ref, main_ref, next_ref, gate_ref, cw_ref, prm_ref,
                mw_ref, cm_ref, qt_ref, ol_ref, gm_ref, xe_ref, *, nck):
    tc = nck * CHUNK
    si = pl.program_id(1)
    ns = pl.num_programs(1)
    hh = GDN_HEADS
    xe_ref[0:8, :] = jnp.where(si > 0, prev_ref[0], 0.0)
    xe_ref[8:8 + tc, :] = main_ref[0]
    xe_ref[8 + tc:16 + tc, :] = jnp.where(si < ns - 1, next_ref[0], 0.0)
    pad = CONV_W // 2
    acc = cw_ref[0:1, :] * xe_ref[8 - pad:8 - pad + tc, :]
    for j in range(1, CONV_W):
        acc = acc + cw_ref[j:j + 1, :] * xe_ref[8 - pad + j:8 - pad + j + tc, :]
    qkv = _silu(acc)
    gs = gate_ref[0]
    a = gs + prm_ref[1:2, :]
    softplus = jnp.maximum(a, 0.0) + jnp.log1p(jnp.exp(-jnp.abs(a)))
    glog = -jnp.exp(prm_ref[0:1, :]) * softplus
    beta = _sigmoid(gs)

    row = lax.broadcasted_iota(jnp.int32, (CHUNK, CHUNK), 0)
    col = lax.broadcasted_iota(jnp.int32, (CHUNK, CHUNK), 1)
    low_bf = (row >= col).astype(BF16)
    up_bf = (row <= col).astype(BF16)

    for c in range(nck):
        r0 = c * CHUNK
        gc = glog[r0:r0 + CHUNK, :]
        gct = gc.T
        gcol = (_tri_sum(low_bf, gc, True), _tri_sum(up_bf, gc, True))
        grow = (_tri_sum(up_bf, gct, False), _tri_sum(low_bf, gct, False))
        for h in range(hh):
            qh = qkv[r0:r0 + CHUNK, h * GDN_DK:(h + 1) * GDN_DK]
            kh = qkv[r0:r0 + CHUNK, (hh + h) * GDN_DK:(hh + h + 1) * GDN_DK]
            vh = qkv[r0:r0 + CHUNK, 2 * hh * GDN_DK + h * GDN_DV:2 * hh * GDN_DK + (h + 1) * GDN_DV]
            qh = qh * lax.rsqrt(jnp.sum(qh * qh, axis=-1, keepdims=True) + EPS) * (GDN_DK ** -0.5)
            kh = kh * lax.rsqrt(jnp.sum(kh * kh, axis=-1, keepdims=True) + EPS)
            kb = kh.astype(BF16)
            kk = lax.dot_general(kb, kb, (((1,), (1,)), ((), ())), preferred_element_type=F32)
            qk = lax.dot_general(qh.astype(BF16), kb, (((1,), (1,)), ((), ())), preferred_element_type=F32)
            for d in range(2):
                idx = d * hh + h
                g_c = gcol[d][:, idx:idx + 1]
                g_r = grow[d][idx:idx + 1, :]
                incl = (row >= col) if d == 0 else (row <= col)
                strict = (row > col) if d == 0 else (row < col)
                decay = jnp.where(incl, jnp.exp(jnp.where(incl, g_c - g_r, 0.0)), 0.0)
                b_c = beta[r0:r0 + CHUNK, 2 * hh + idx:2 * hh + idx + 1]
                tinv = _tri_inverse(jnp.where(strict, b_c * kk * decay, 0.0), row, col)
                eg = jnp.exp(g_c)
                rhs = jnp.concatenate([vh * b_c, kh * (b_c * eg)], axis=1)
                uw = _dot_x3(tinv, rhs)
                uwb = uw.astype(BF16)
                attn = jnp.where(incl, qk * decay, 0.0)
                g_tot = g_c[CHUNK - 1:CHUNK, :] if d == 0 else g_c[0:1, :]
                ktail = kh * jnp.exp(g_tot - g_c)
                au = _dot(attn.astype(BF16), uwb)
                kt = _dot(ktail.T.astype(BF16), uwb)
                ol_ref[0, c, d, h] = au[:, 0:GDN_DV]
                qt_ref[0, c, d, h] = qh * eg - au[:, GDN_DV:]
                cm_ref[0, c, d, h] = kt[:, 0:GDN_DV]
                mw_ref[0, c, d, h] = kt[:, GDN_DV:]
                gm_ref[0, c, idx:idx + 1, :] = jnp.broadcast_to(jnp.exp(g_tot), (1, 128))


def _gdn_ops(qkv_raw, gates, cw8, prm, nck):
    b, s, _ = qkv_raw.shape
    tc = nck * CHUNK
    n = s // CHUNK
    r8 = tc // 8
    nb8 = s // 8
    op_shape = jax.ShapeDtypeStruct((b, n, 2, GDN_HEADS, CHUNK, 128), F32)
    op_spec = pl.BlockSpec((1, nck, 2, GDN_HEADS, CHUNK, 128), lambda bi, si: (bi, si, 0, 0, 0, 0))
    return pl.pallas_call(
        functools.partial(_gdn_kernel, nck=nck),
        out_shape=(op_shape, op_shape, op_shape, op_shape,
                   jax.ShapeDtypeStruct((b, n, 8, 128), F32)),
        grid=(b, s // tc),
        in_specs=[pl.BlockSpec((1, 8, GDN_CH), lambda bi, si: (bi, jnp.maximum(si * r8 - 1, 0), 0)),
                  pl.BlockSpec((1, tc, GDN_CH), lambda bi, si: (bi, si, 0)),
                  pl.BlockSpec((1, 8, GDN_CH), lambda bi, si: (bi, jnp.minimum((si + 1) * r8, nb8 - 1), 0)),
                  pl.BlockSpec((1, tc, GATE_PAD), lambda bi, si: (bi, si, 0)),
                  pl.BlockSpec((8, GDN_CH), lambda bi, si: (0, 0)),
                  pl.BlockSpec((8, 128), lambda bi, si: (0, 0))],
        out_specs=(op_spec, op_spec, op_spec, op_spec,
                   pl.BlockSpec((1, nck, 8, 128), lambda bi, si: (bi, si, 0, 0))),
        scratch_shapes=[pltpu.VMEM((tc + 16, GDN_CH), F32)],
        compiler_params=_cparams(("parallel", "parallel")),
        name="gdn_ops",
    )(qkv_raw, qkv_raw, qkv_raw, gates, cw8, prm)


def _scan_kernel(mwf, cmf, qtf, olf, gmf, mwb, cmb, qtb, olb, gmb, of_ref, ob_ref, s_ref):
    @pl.when(pl.program_id(1) == 0)
    def _():
        s_ref[...] = jnp.zeros_like(s_ref)

    hh = GDN_HEADS
    for d, (mw, cm, qt, ol, gm, o_ref) in enumerate(((mwf, cmf, qtf, olf, gmf, of_ref),
                                                     (mwb, cmb, qtb, olb, gmb, ob_ref))):
        for h in range(hh):
            idx = d * hh + h
            st = s_ref[idx]
            sb = st.astype(BF16)
            o_ref[0, :, h * GDN_DV:(h + 1) * GDN_DV] = _dot(qt[0, 0, 0, h].astype(BF16), sb) + ol[0, 0, 0, h]
            gam = gm[0, 0, idx:idx + 1, :]
            s_ref[idx] = st * gam - _dot(mw[0, 0, 0, h].astype(BF16), sb) + cm[0, 0, 0, h]


def _gdn_scan(mw, cm, qt, ol, gm):
    b, n = mw.shape[0], mw.shape[1]
    s = n * CHUNK
    blk = (1, 1, 1, GDN_HEADS, CHUNK, 128)
    fwd = lambda bi, ni: (bi, ni, 0, 0, 0, 0)
    bwd = lambda bi, ni: (bi, n - 1 - ni, 1, 0, 0, 0)
    gfwd = lambda bi, ni: (bi, ni, 0, 0)
    gbwd = lambda bi, ni: (bi, n - 1 - ni, 0, 0)
    o_shape = jax.ShapeDtypeStruct((b, s, GDN_W), F32)
    return pl.pallas_call(
        _scan_kernel,
        out_shape=(o_shape, o_shape),
        grid=(b, n),
        in_specs=[pl.BlockSpec(blk, fwd)] * 4 + [pl.BlockSpec((1, 1, 8, 128), gfwd)]
        + [pl.BlockSpec(blk, bwd)] * 4 + [pl.BlockSpec((1, 1, 8, 128), gbwd)],
        out_specs=(pl.BlockSpec((1, CHUNK, GDN_W), lambda bi, ni: (bi, ni, 0)),
                   pl.BlockSpec((1, CHUNK, GDN_W), lambda bi, ni: (bi, n - 1 - ni, 0))),
        scratch_shapes=[pltpu.VMEM((2 * GDN_HEADS, GDN_DK, GDN_DV), F32)],
        compiler_params=_cparams(("parallel", "arbitrary")),
        name="gdn_scan",
    )(mw, cm, qt, ol, gm, mw, cm, qt, ol, gm)


def _post_kernel(x_ref, mod_ref, modf_ref, om_ref, of_ref, ob_ref, z_ref, ggdn_ref, wout_ref,
                 gmlp_ref, w1_ref, w2_ref, gfin_ref, y_ref, *, ff_blk):
    d = D_MODEL
    x = x_ref[0]
    mod = mod_ref[0]
    gate_a = mod[:, 2 * d:3 * d]
    shift_m = mod[:, 3 * d:4 * d]
    scale_m = mod[:, 4 * d:5 * d]
    gate_m = mod[:, 5 * d:6 * d]
    modf = modf_ref[0]
    shift_f = modf[:, 0:d]
    scale_f = modf[:, d:2 * d]

    o = of_ref[0] + ob_ref[0]
    z = z_ref[0]
    mixed = _dot(om_ref[0], wout_ref[0:MLA_W, :])
    for h in range(GDN_HEADS):
        sl = slice(h * GDN_DV, (h + 1) * GDN_DV)
        og = _rms(o[:, sl], ggdn_ref[...]) * _silu(z[:, sl])
        mixed = mixed + _dot(og.astype(BF16), wout_ref[MLA_W + h * GDN_DV:MLA_W + (h + 1) * GDN_DV, :])
    x1 = x + gate_a * mixed
    hb = (_rms(x1, gmlp_ref[...]) * (1.0 + scale_m) + shift_m).astype(BF16)
    acc = jnp.zeros_like(x1)
    for j in range(D_FF // ff_blk):
        hid = _dot(hb, w1_ref[:, j * ff_blk:(j + 1) * ff_blk])
        hid = jnp.square(jnp.maximum(hid, 0.0)).astype(BF16)
        acc = acc + _dot(hid, w2_ref[j * ff_blk:(j + 1) * ff_blk, :])
    x2 = x1 + gate_m * acc
    y_ref[0] = _rms(x2, gfin_ref[...]) * (1.0 + scale_f) + shift_f


def _post(x, mod, modf, om, of, ob, z, ggdn, wout, gmlp, w1, w2, gfin, ts):
    b, s, d = x.shape
    const = lambda bi, si: (0, 0)
    tok = lambda w: pl.BlockSpec((1, ts, w), lambda bi, si: (bi, si, 0))
    one = pl.Buffered(1)
    return pl.pallas_call(
        functools.partial(_post_kernel, ff_blk=1024),
        out_shape=jax.ShapeDtypeStruct((b, s, d), F32),
        grid=(b, s // ts),
        in_specs=[tok(d),
                  pl.BlockSpec((1, 1, N_MOD * d), lambda bi, si: (bi, 0, 0)),
                  pl.BlockSpec((1, 1, 2 * d), lambda bi, si: (bi, 0, 0)),
                  tok(MLA_W), tok(GDN_W), tok(GDN_W), tok(GDN_W),
                  pl.BlockSpec((1, GDN_DV), const),
                  pl.BlockSpec((MLA_W + GDN_W, d), const, pipeline_mode=one),
                  pl.BlockSpec((1, d), const),
                  pl.BlockSpec((d, D_FF), const, pipeline_mode=one),
                  pl.BlockSpec((D_FF, d), const, pipeline_mode=one),
                  pl.BlockSpec((1, d), const)],
        out_specs=tok(d),
        compiler_params=_cparams(("parallel", "parallel")),
        name="post",
    )(x, mod, modf, om, of, ob, z, ggdn, wout, gmlp, w1, w2, gfin)


def _swap_halves(t):
    half = t.shape[-1] // 2
    return jnp.concatenate([t[..., half:], t[..., :half]], axis=-1)


def _rope_tables(s):
    inv = 1.0 / (ROPE_THETA ** (jnp.arange(0, QK_ROPE, 2, dtype=F32) / QK_ROPE))
    ang = jnp.arange(s, dtype=F32)[:, None] * inv[None, :]
    cos, sin = jnp.cos(ang), jnp.sin(ang)
    zeros = jnp.zeros((s, 128 - QK_ROPE), F32)
    return (jnp.concatenate([cos, cos, zeros], axis=1),
            jnp.concatenate([-sin, sin, zeros], axis=1))


def _pick(n, pref):
    t = min(n, pref)
    while n % t:
        t //= 2
    return t


def _trunk(x, mod, modf, w):
    b, s, d = x.shape
    cs, sn = _rope_tables(s)
    ts = _pick(s, 256)
    q, k, v, qkv_raw, z, gates = _pre(x, mod, w["gmix"], w["wa"], w["gq"], w["wuq"], w["gkv"], w["wukv"],
                                      cs, sn, ts)
    o_mla = _attention(q, k, v, _pick(s, 512), _pick(s, 512))
    nck = 2 if s % (2 * CHUNK) == 0 else 1
    mw, cm, qt, ol, gm = _gdn_ops(qkv_raw, gates, w["cw8"], w["prm"], nck)
    o_f, o_b = _gdn_scan(mw, cm, qt, ol, gm)
    return _post(x, mod, modf, o_mla, o_f, o_b, z, w["ggdn"], w["wout"], w["gmlp"], w["w1"], w["w2"],
                 w["gfin"], ts)


def kernel(x_prompt, x_sample, c_prompt, c_sample, w_ada, b_ada, g_mix, w_in, g_q, w_uq, g_kv, w_ukv,
           conv_w, a_log_f, a_log_b, dt_f, dt_b, g_gdn, w_out, g_mlp, w_mlp_in, w_mlp_out,
           w_ada_f, b_ada_f, g_final):
    d = D_MODEL
    bp, bs = x_prompt.shape[0], x_sample.shape[0]
    c8 = jnp.zeros((8, d), F32).at[:bp].set(c_prompt).at[bp:bp + bs].set(c_sample)
    mod = _ada(c8, w_ada[0], b_ada[0][None, :])[:, None, :]
    modf = _ada(c8, w_ada_f, b_ada_f[None, :])[:, None, :]

    wi = w_in[0]
    o_kr = Q_LORA + KV_LORA
    o_qkv = o_kr + QK_ROPE
    o_z = o_qkv + GDN_CH
    o_g = o_z + GDN_W
    kr = wi[:, o_kr:o_qkv]
    wa = jnp.concatenate([wi[:, :o_kr], kr, _swap_halves(kr), wi[:, o_qkv:o_g], wi[:, o_g:],
                          jnp.zeros((d, GATE_PAD - 4 * GDN_HEADS), F32)], axis=1).astype(BF16)
    wq = w_uq[0].reshape(Q_LORA, MLA_HEADS, QK_NOPE + QK_ROPE)
    wq_r = wq[..., QK_NOPE:]
    wuq = jnp.concatenate([wq[..., :QK_NOPE], wq_r, _swap_halves(wq_r)], axis=-1)
    wuq = wuq.reshape(Q_LORA, MLA_HEADS * QK_PAD).astype(BF16)
    lanes = jnp.zeros((128,), F32)
    prm = jnp.zeros((8, 128), F32)
    prm = prm.at[0].set(lanes.at[0:4].set(a_log_f[0]).at[4:8].set(a_log_b[0]))
    prm = prm.at[1].set(lanes.at[0:4].set(dt_f[0]).at[4:8].set(dt_b[0]))
    w = dict(
        gmix=g_mix, wa=wa, gq=g_q, wuq=wuq, gkv=g_kv, wukv=w_ukv[0].astype(BF16),
        cw8=jnp.zeros((8, GDN_CH), F32).at[:CONV_W].set(conv_w[0]), prm=prm,
        ggdn=g_gdn, wout=w_out[0].astype(BF16), gmlp=g_mlp,
        w1=w_mlp_in[0].astype(BF16), w2=w_mlp_out[0].astype(BF16), gfin=g_final[None, :],
    )
    y_p = _trunk(x_prompt, mod[:bp], modf[:bp], w)
    y_s = _trunk(x_sample, mod[bp:bp + bs], modf[bp:bp + bs], w)
    return (y_p, y_s)
```

```python
import functools

import jax
import jax.numpy as jnp
from jax import lax
from jax.experimental import pallas as pl
from jax.experimental.pallas import tpu as pltpu

F32 = jnp.float32
BF16 = jnp.bfloat16

EPS = 1e-6
LOG2_E = 1.4426950408889634
D_MODEL = 1024
N_MOD = 6
MLA_HEADS = 4
QK_NOPE = 128
QK_ROPE = 64
V_DIM = 128
Q_LORA = 384
KV_LORA = 256
ROPE_THETA = 10000.0
GDN_HEADS = 4
GDN_DK = 128
GDN_DV = 128
CONV_W = 5
D_FF = 4 * D_MODEL
QK_PAD = 256
GDN_CH = 2 * GDN_HEADS * GDN_DK + GDN_HEADS * GDN_DV
GDN_W = GDN_HEADS * GDN_DV
MLA_W = MLA_HEADS * V_DIM
GATE_PAD = 128
C_CQ = 0
C_CKV = C_CQ + Q_LORA
C_KR = C_CKV + KV_LORA
C_QKV = C_KR + 128
C_Z = C_QKV + GDN_CH
C_GATE = C_Z + GDN_W
WA_COLS = C_GATE + GATE_PAD

CHUNK = 128
INV_BLK = 16
VMEM_LIMIT = 56 * 1024 * 1024


def _dot(a, b):
    return jnp.dot(a, b, preferred_element_type=F32)


def _dot_bf(a, b):
    return jnp.dot(a.astype(BF16), b.astype(BF16), preferred_element_type=F32)


def _split2(a):
    hi = a.astype(BF16)
    lo = (a - hi.astype(F32)).astype(BF16)
    return hi, lo


def _dot_x3(a, b):
    ah, al = _split2(a)
    bh, bl = _split2(b)
    return _dot(ah, bh) + (_dot(ah, bl) + _dot(al, bh))


def _sigmoid(x):
    return 1.0 / (1.0 + jnp.exp(-x))


def _silu(x):
    return x * _sigmoid(x)


def _cparams(sem):
    return pltpu.CompilerParams(dimension_semantics=sem, vmem_limit_bytes=VMEM_LIMIT)


def _ada_kernel(c_ref, w_ref, b_ref, o_ref):
    sc = _silu(c_ref[...])
    o_ref[...] = _dot_bf(sc, w_ref[...]) + b_ref[...]


def _ada(c8, w, b):
    d, n = w.shape
    tn = 1024
    return pl.pallas_call(
        _ada_kernel,
        out_shape=jax.ShapeDtypeStruct((8, n), F32),
        grid=(n // tn,),
        in_specs=[pl.BlockSpec((8, d), lambda j: (0, 0)),
                  pl.BlockSpec((d, tn), lambda j: (0, j)),
                  pl.BlockSpec((1, tn), lambda j: (0, j))],
        out_specs=pl.BlockSpec((8, tn), lambda j: (0, j)),
        compiler_params=_cparams(("parallel",)),
        name="ada",
    )(c8, w, b)


def _rms(x, g):
    ms = jnp.mean(x * x, axis=-1, keepdims=True)
    return x * lax.rsqrt(ms + EPS) * g


def _pre_kernel(x_ref, mod_ref, gmix_ref, wa_ref, gq_ref, wuq_ref, gkv_ref, wukv_ref, cs_ref, sn_ref,
                q_ref, k_ref, v_ref, qkv_ref, z_ref, gate_ref):
    d = D_MODEL
    x = x_ref[0]
    mod = mod_ref[0]
    shift = mod[:, 0:d]
    scale = mod[:, d:2 * d]
    hb = (_rms(x, gmix_ref[...]) * (1.0 + scale) + shift).astype(BF16)

    def proj(lo, hi):
        return _dot(hb, wa_ref[:, lo:hi])

    cs = cs_ref[...]
    sn = sn_ref[...]

    def rope(t):
        return t * cs + pltpu.roll(t, 64, 1) * sn

    qk_scale = (QK_NOPE + QK_ROPE) ** -0.5 * LOG2_E
    ones = jnp.ones((x.shape[0], V_DIM), BF16)
    cq =_rms(proj(C_CQ, C_CKV), gq_ref[...])
    q = _dot_bf(cq, wuq_ref[...])
    ckv = _rms(proj(C_CKV, C_KR), gkv_ref[...])
    kv = _dot_bf(ckv, wukv_ref[...])
    kr = rope(proj(C_KR, C_QKV)).astype(BF16)
    for h in range(MLA_HEADS):
        o = h * QK_PAD
        q_ref[0, h, :, 0:128] = (q[:, o:o + 128] * qk_scale).astype(BF16)
        q_ref[0, h, :, 128:256] = (rope(q[:, o + 128:o + 256]) * qk_scale).astype(BF16)
        k_ref[0, h, :, 0:128] = kv[:, o:o + 128].astype(BF16)
        k_ref[0, h, :, 128:256] = kr
        v_ref[0, h, :, 0:V_DIM] = kv[:, o + 128:o + 256].astype(BF16)
        v_ref[0, h, :, V_DIM:2 * V_DIM] = ones
    qkv_ref[0] = proj(C_QKV, C_Z)
    z_ref[0] = proj(C_Z, C_GATE)
    gate_ref[0] = proj(C_GATE, WA_COLS)


def _pre(x, mod, gmix, wa, gq, wuq, gkv, wukv, cs, sn, ts):
    b, s, d = x.shape
    h = MLA_HEADS
    const = lambda bi, si: (0, 0)
    return pl.pallas_call(
        _pre_kernel,
        out_shape=(jax.ShapeDtypeStruct((b, h, s, QK_PAD), BF16),
                   jax.ShapeDtypeStruct((b, h, s, QK_PAD), BF16),
                   jax.ShapeDtypeStruct((b, h, s, 2 * V_DIM), BF16),
                   jax.ShapeDtypeStruct((b, s, GDN_CH), F32),
                   jax.ShapeDtypeStruct((b, s, GDN_W), F32),
                   jax.ShapeDtypeStruct((b, s, GATE_PAD), F32)),
        grid=(b, s // ts),
        in_specs=[pl.BlockSpec((1, ts, d), lambda bi, si: (bi, si, 0)),
                  pl.BlockSpec((1, 1, N_MOD * d), lambda bi, si: (bi, 0, 0)),
                  pl.BlockSpec((1, d), const),
                  pl.BlockSpec((d, WA_COLS), const),
                  pl.BlockSpec((1, Q_LORA), const),
                  pl.BlockSpec((Q_LORA, h * QK_PAD), const),
                  pl.BlockSpec((1, KV_LORA), const),
                  pl.BlockSpec((KV_LORA, h * 256), const),
                  pl.BlockSpec((ts, 128), lambda bi, si: (si, 0)),
                  pl.BlockSpec((ts, 128), lambda bi, si: (si, 0))],
        out_specs=(pl.BlockSpec((1, h, ts, QK_PAD), lambda bi, si: (bi, 0, si, 0)),
                   pl.BlockSpec((1, h, ts, QK_PAD), lambda bi, si: (bi, 0, si, 0)),
                   pl.BlockSpec((1, h, ts, 2 * V_DIM), lambda bi, si: (bi, 0, si, 0)),
                   pl.BlockSpec((1, ts, GDN_CH), lambda bi, si: (bi, si, 0)),
                   pl.BlockSpec((1, ts, GDN_W), lambda bi, si: (bi, si, 0)),
                   pl.BlockSpec((1, ts, GATE_PAD), lambda bi, si: (bi, si, 0))),
        compiler_params=_cparams(("parallel", "parallel")),
        name="pre",
    )(x, mod, gmix, wa, gq, wuq, gkv, wukv, cs, sn)


def _attn_kernel(q_ref, k_ref, v_ref, o_ref, s_ref, p_ref, acc_ref, m_ref, al_ref, *, tk, nk):
    q = q_ref[0, 0]

    def scores(i):
        start = pl.multiple_of(i * tk, tk)
        k = k_ref[0, 0, pl.ds(start, tk), :]
        return lax.dot_general(q, k, (((1,), (1,)), ((), ())), preferred_element_type=F32)

    def flush(i, slot):
        v = v_ref[0, 0, pl.ds(pl.multiple_of(i * tk, tk), tk), :]
        acc_ref[...] = al_ref[...] * acc_ref[...] + _dot(p_ref[slot], v)

    def step(i, cur):
        flush(jnp.maximum(i - 1, 0), 1 - cur)
        s_ref[1 - cur] = scores(jnp.minimum(i + 1, nk - 1))
        s = s_ref[cur]
        m = m_ref[...]
        m_new = jnp.maximum(m, jnp.max(s, axis=-1, keepdims=True))
        al_ref[...] = jnp.exp2(m - m_new)
        m_ref[...] = m_new
        p_ref[cur] = jnp.exp2(s - m_new).astype(BF16)

    m_ref[...] = jnp.full(m_ref.shape, -jnp.inf, F32)
    al_ref[...] = jnp.zeros(al_ref.shape, F32)
    acc_ref[...] = jnp.zeros(acc_ref.shape, F32)
    p_ref[1] = jnp.zeros(p_ref.shape[1:], BF16)
    s_ref[0] = scores(0)

    def pair(j, carry):
        step(2 * j, 0)
        step(2 * j + 1, 1)
        return carry

    lax.fori_loop(0, nk // 2, pair, 0)
    flush(nk - 1, 1)
    acc = acc_ref[...]
    o_ref[0] = (acc[:, :V_DIM] / acc[:, V_DIM:]).astype(o_ref.dtype)


def _attention(q, k, v, tq, tk):
    b, h, s, _ = q.shape
    assert (s // tk) % 2 == 0
    return pl.pallas_call(
        functools.partial(_attn_kernel, tk=tk, nk=s // tk),
        out_shape=jax.ShapeDtypeStruct((b, s, h * V_DIM), BF16),
        grid=(b, h, s // tq),
        in_specs=[pl.BlockSpec((1, 1, tq, QK_PAD), lambda bi, hi, qi: (bi, hi, qi, 0)),
                  pl.BlockSpec((1, 1, s, QK_PAD), lambda bi, hi, qi: (bi, hi, 0, 0)),
                  pl.BlockSpec((1, 1, s, 2 * V_DIM), lambda bi, hi, qi: (bi, hi, 0, 0))],
        out_specs=pl.BlockSpec((1, tq, V_DIM), lambda bi, hi, qi: (bi, qi, hi)),
        scratch_shapes=[pltpu.VMEM((2, tq, tk), F32), pltpu.VMEM((2, tq, tk), BF16),
                        pltpu.VMEM((tq, 2 * V_DIM), F32), pltpu.VMEM((tq, 1), F32), pltpu.VMEM((tq, 1), F32)],
        compiler_params=_cparams(("parallel", "parallel", "parallel")),
        name="attn",
    )(q, k, v)


def _bf(x):
    return x.astype(BF16)


def _cat(a, b):
    return jnp.concatenate([a, b], axis=1)


def _tri_solve_many(a_list, rhs_list, row, col):
    c = CHUNK
    eye = (row == col).astype(F32)
    blk = (row // INV_BLK) == (col // INV_BLK)
    n = range(len(a_list))
    x0 = [jnp.where(blk, -a, 0.0) for a in a_list]
    e = [_bf(jnp.where(blk, 0.0, a)) for a in a_list]
    xb = [_bf(x) for x in x0]
    x1 = [_dot(xb[i], xb[i]) for i in n]
    s = [eye + x0[i] for i in n]
    xb = [_bf(x) for x in x1]
    r = [_dot(xb[i], _cat(xb[i], _bf(s[i]))) for i in n]
    s = [s[i] + r[i][:, c:] for i in n]
    xb = [_bf(r[i][:, :c]) for i in n]
    r = [_dot(xb[i], _cat(xb[i], _bf(s[i]))) for i in n]
    s = [s[i] + r[i][:, c:] for i in n]
    xb = [_bf(r[i][:, :c]) for i in n]
    sb = [_bf(s[i]) for i in n]
    s = [s[i] + _dot(xb[i], sb[i]) for i in n]
    sb = [_bf(s[i]) for i in n]
    y0 = [-_dot(sb[i], e[i]) for i in n]
    dr = [_dot(sb[i], _bf(rhs_list[i])) for i in n]
    yb = [_bf(y) for y in y0]
    y1 = [_dot(yb[i], yb[i]) for i in n]
    t = [eye + y0[i] for i in n]
    yb = [_bf(y) for y in y1]
    r = [_dot(yb[i], _cat(yb[i], _bf(t[i]))) for i in n]
    t = [t[i] + r[i][:, c:] for i in n]
    yb = [_bf(r[i][:, :c]) for i in n]
    tb = [_bf(t[i]) for i in n]
    t = [t[i] + _dot(yb[i], tb[i]) for i in n]
    return [_dot(_bf(t[i]), _bf(dr[i])) for i in n]


def _gdn_kernel(prev_ref, main_ref, next_ref, gate_ref, cw_ref, prm_ref,
                mw_ref, cm_ref, qt_ref, ol_ref, gm_ref, xe_ref, *, nck):
    tc = nck * CHUNK
    si = pl.program_id(1)
    ns = pl.num_programs(1)
    hh = GDN_HEADS
    xe_ref[0:8, :] = jnp.where(si > 0, prev_ref[0], 0.0)
    xe_ref[8:8 + tc, :] = main_ref[0]
    xe_ref[8 + tc:16 + tc, :] = jnp.where(si < ns - 1, next_ref[0], 0.0)
    pad = CONV_W // 2
    acc = cw_ref[0:1, :] * xe_ref[8 - pad:8 - pad + tc, :]
    for j in range(1, CONV_W):
        acc = acc + cw_ref[j:j + 1, :] * xe_ref[8 - pad + j:8 - pad + j + tc, :]
    qkv = _silu(acc)
    gs = gate_ref[0]
    a = gs + prm_ref[1:2, :]
    softplus = jnp.maximum(a, 0.0) + jnp.log1p(jnp.exp(-jnp.abs(a)))
    lane = lax.broadcasted_iota(jnp.int32, (tc, 128), 1)
    glog = jnp.where(lane < 2 * hh, -jnp.exp(prm_ref[0:1, :]) * softplus, 0.0)
    beta = _sigmoid(gs)
    g1 = _bf(glog).astype(F32)
    r1 = glog - g1
    g2 = _bf(r1).astype(F32)
    g3 = r1 - g2
    gpk = _bf(g1 + pltpu.roll(g2, 8, 1) + pltpu.roll(g3, 16, 1))

    row = lax.broadcasted_iota(jnp.int32, (CHUNK, CHUNK), 0)
    col = lax.broadcasted_iota(jnp.int32, (CHUNK, CHUNK), 1)
    low_bf = (row >= col).astype(BF16)
    up_bf = (row <= col).astype(BF16)
    masks = ((row >= col, row > col), (row <= col, row < col))

    for c in range(nck):
        r0 = c * CHUNK
        pk = gpk[r0:r0 + CHUNK, :]
        pre = _dot(low_bf, pk)
        suf = _dot(up_bf, pk)
        gsum = jnp.where(col < hh, pre, suf)
        gsum = gsum + pltpu.roll(gsum, 120, 1) + pltpu.roll(gsum, 112, 1)
        gsum_t = gsum.T
        a_list, rhs_list, inst = [], [], []
        for h in range(hh):
            qh = qkv[r0:r0 + CHUNK, h * GDN_DK:(h + 1) * GDN_DK]
            kh = qkv[r0:r0 + CHUNK, (hh + h) * GDN_DK:(hh + h + 1) * GDN_DK]
            vh = qkv[r0:r0 + CHUNK, 2 * hh * GDN_DK + h * GDN_DV:2 * hh * GDN_DK + (h + 1) * GDN_DV]
            qh = qh * (lax.rsqrt(jnp.sum(qh * qh, axis=-1, keepdims=True) + EPS) * (GDN_DK ** -0.5))
            kh = kh * lax.rsqrt(jnp.sum(kh * kh, axis=-1, keepdims=True) + EPS)
            kb = _bf(kh)
            gram = lax.dot_general(jnp.concatenate([kb, _bf(qh)], axis=0), kb,
                                   (((1,), (1,)), ((), ())), preferred_element_type=F32)
            kk = gram[:CHUNK]
            qk = gram[CHUNK:]
            for d in range(2):
                idx = d * hh + h
                incl, strict = masks[d]
                g_c = gsum[:, idx:idx + 1]
                g_r = gsum_t[idx:idx + 1, :]
                decay = jnp.where(incl, jnp.exp(jnp.where(incl, g_c - g_r, 0.0)), 0.0)
                b_c = beta[r0:r0 + CHUNK, 2 * hh + idx:2 * hh + idx + 1]
                eg = jnp.exp(g_c)
                g_tot = g_c[CHUNK - 1:CHUNK, :] if d == 0 else g_c[0:1, :]
                a_list.append(jnp.where(strict, (b_c * kk) * decay, 0.0))
                rhs_list.append(_cat(vh * b_c, kh * (b_c * eg)))
                inst.append((h, d, idx, _bf(jnp.where(incl, qk * decay, 0.0)),
                             _bf((kh * jnp.exp(g_tot - g_c)).T), qh * eg, jnp.exp(g_tot)))
        uw = _tri_solve_many(a_list, rhs_list, row, col)
        uwb = [_bf(x) for x in uw]
        au = [_dot(inst[i][3], uwb[i]) for i in range(len(inst))]
        kt = [_dot(inst[i][4], uwb[i]) for i in range(len(inst))]
        for i, (h, d, idx, _, _, qdec, gam) in enumerate(inst):
            ol_ref[0, c, d, h] = au[i][:, 0:GDN_DV]
            qt_ref[0, c, d, h] = qdec - au[i][:, GDN_DV:]
            cm_ref[0, c, d, h] = kt[i][:, 0:GDN_DV]
            mw_ref[0, c, d, h] = kt[i][:, GDN_DV:]
            gm_ref[0, c, idx:idx + 1, :] = jnp.broadcast_to(gam, (1, 128))


def _gdn_ops(qkv_raw, gates, cw8, prm, nck):
    b, s, _ = qkv_raw.shape
    tc = nck * CHUNK
    n = s // CHUNK
    r8 = tc // 8
    nb8 = s // 8
    op_shape = jax.ShapeDtypeStruct((b, n, 2, GDN_HEADS, CHUNK, 128), F32)
    op_spec = pl.BlockSpec((1, nck, 2, GDN_HEADS, CHUNK, 128), lambda bi, si: (bi, si, 0, 0, 0, 0))
    return pl.pallas_call(
        functools.partial(_gdn_kernel, nck=nck),
        out_shape=(op_shape, op_shape, op_shape, op_shape,
                   jax.ShapeDtypeStruct((b, n, 8, 128), F32)),
        grid=(b, s // tc),
        in_specs=[pl.BlockSpec((1, 8, GDN_CH), lambda bi, si: (bi, jnp.maximum(si * r8 - 1, 0), 0)),
                  pl.BlockSpec((1, tc, GDN_CH), lambda bi, si: (bi, si, 0)),
                  pl.BlockSpec((1, 8, GDN_CH), lambda bi, si: (bi, jnp.minimum((si + 1) * r8, nb8 - 1), 0)),
                  pl.BlockSpec((1, tc, GATE_PAD), lambda bi, si: (bi, si, 0)),
                  pl.BlockSpec((8, GDN_CH), lambda bi, si: (0, 0)),
                  pl.BlockSpec((8, 128), lambda bi, si: (0, 0))],
        out_specs=(op_spec, op_spec, op_spec, op_spec,
                   pl.BlockSpec((1, nck, 8, 128), lambda bi, si: (bi, si, 0, 0))),
        scratch_shapes=[pltpu.VMEM((tc + 16, GDN_CH), F32)],
        compiler_params=_cparams(("parallel", "parallel")),
        name="gdn_ops",
    )(qkv_raw, qkv_raw, qkv_raw, gates, cw8, prm)


def _scan_kernel(mwf, cmf, qtf, olf, gmf, mwb, cmb, qtb, olb, gmb, of_ref, ob_ref, s_ref):
    @pl.when(pl.program_id(1) == 0)
    def _():
        s_ref[...] = jnp.zeros_like(s_ref)

    hh = GDN_HEADS
    for d, (mw, cm, qt, ol, gm, o_ref) in enumerate(((mwf, cmf, qtf, olf, gmf, of_ref),
                                                     (mwb, cmb, qtb, olb, gmb, ob_ref))):
        for h in range(hh):
            idx = d * hh + h
            st = s_ref[idx]
            sb = st.astype(BF16)
            o_ref[0, :, h * GDN_DV:(h + 1) * GDN_DV] = _dot(qt[0, 0, 0, h].astype(BF16), sb) + ol[0, 0, 0, h]
            gam = gm[0, 0, idx:idx + 1, :]
            s_ref[idx] = st * gam - _dot(mw[0, 0, 0, h].astype(BF16), sb) + cm[0, 0, 0, h]


def _gdn_scan(mw, cm, qt, ol, gm):
    b, n = mw.shape[0], mw.shape[1]
    s = n * CHUNK
    blk = (1, 1, 1, GDN_HEADS, CHUNK, 128)
    fwd = lambda bi, ni: (bi, ni, 0, 0, 0, 0)
    bwd = lambda bi, ni: (bi, n - 1 - ni, 1, 0, 0, 0)
    gfwd = lambda bi, ni: (bi, ni, 0, 0)
    gbwd = lambda bi, ni: (bi, n - 1 - ni, 0, 0)
    o_shape = jax.ShapeDtypeStruct((b, s, GDN_W), F32)
    return pl.pallas_call(
        _scan_kernel,
        out_shape=(o_shape, o_shape),
        grid=(b, n),
        in_specs=[pl.BlockSpec(blk, fwd)] * 4 + [pl.BlockSpec((1, 1, 8, 128), gfwd)]
        + [pl.BlockSpec(blk, bwd)] * 4 + [pl.BlockSpec((1, 1, 8, 128), gbwd)],
        out_specs=(pl.BlockSpec((1, CHUNK, GDN_W), lambda bi, ni: (bi, ni, 0)),
                   pl.BlockSpec((1, CHUNK, GDN_W), lambda bi, ni: (bi, n - 1 - ni, 0))),
        scratch_shapes=[pltpu.VMEM((2 * GDN_HEADS, GDN_DK, GDN_DV), F32)],
        compiler_params=_cparams(("parallel", "arbitrary")),
        name="gdn_scan",
    )(mw, cm, qt, ol, gm, mw, cm, qt, ol, gm)


def _post_kernel(x_ref, mod_ref, modf_ref, om_ref, of_ref, ob_ref, z_ref, ggdn_ref, wout_ref,
                 gmlp_ref, w1_ref, w2_ref, gfin_ref, y_ref, *, ff_blk):
    d = D_MODEL
    x = x_ref[0]
    mod = mod_ref[0]
    gate_a = mod[:, 2 * d:3 * d]
    shift_m = mod[:, 3 * d:4 * d]
    scale_m = mod[:, 4 * d:5 * d]
    gate_m = mod[:, 5 * d:6 * d]
    modf = modf_ref[0]
    shift_f = modf[:, 0:d]
    scale_f = modf[:, d:2 * d]

    o = of_ref[0] + ob_ref[0]
    z = z_ref[0]
    mixed = _dot(om_ref[0], wout_ref[0:MLA_W, :])
    for h in range(GDN_HEADS):
        sl = slice(h * GDN_DV, (h + 1) * GDN_DV)
        og = _rms(o[:, sl], ggdn_ref[...]) * _silu(z[:, sl])
        mixed = mixed + _dot(og.astype(BF16), wout_ref[MLA_W + h * GDN_DV:MLA_W + (h + 1) * GDN_DV, :])
    x1 = x + gate_a * mixed
    hb = (_rms(x1, gmlp_ref[...]) * (1.0 + scale_m) + shift_m).astype(BF16)
    acc = jnp.zeros_like(x1)
    for j in range(D_FF // ff_blk):
        hid = _dot(hb, w1_ref[:, j * ff_blk:(j + 1) * ff_blk])
        hid = jnp.square(jnp.maximum(hid, 0.0)).astype(BF16)
        acc = acc + _dot(hid, w2_ref[j * ff_blk:(j + 1) * ff_blk, :])
    x2 = x1 + gate_m * acc
    y_ref[0] = _rms(x2, gfin_ref[...]) * (1.0 + scale_f) + shift_f


def _post(x, mod, modf, om, of, ob, z, ggdn, wout, gmlp, w1, w2, gfin, ts):
    b, s, d = x.shape
    const = lambda bi, si: (0, 0)
    tok = lambda w: pl.BlockSpec((1, ts, w), lambda bi, si: (bi, si, 0))
    one = pl.Buffered(1)
    return pl.pallas_call(
        functools.partial(_post_kernel, ff_blk=1024),
        out_shape=jax.ShapeDtypeStruct((b, s, d), F32),
        grid=(b, s // ts),
        in_specs=[tok(d),
                  pl.BlockSpec((1, 1, N_MOD * d), lambda bi, si: (bi, 0, 0)),
                  pl.BlockSpec((1, 1, 2 * d), lambda bi, si: (bi, 0, 0)),
                  tok(MLA_W), tok(GDN_W), tok(GDN_W), tok(GDN_W),
                  pl.BlockSpec((1, GDN_DV), const),
                  pl.BlockSpec((MLA_W + GDN_W, d), const, pipeline_mode=one),
                  pl.BlockSpec((1, d), const),
                  pl.BlockSpec((d, D_FF), const, pipeline_mode=one),
                  pl.BlockSpec((D_FF, d), const, pipeline_mode=one),
                  pl.BlockSpec((1, d), const)],
        out_specs=tok(d),
        compiler_params=_cparams(("parallel", "parallel")),
        name="post",
    )(x, mod, modf, om, of, ob, z, ggdn, wout, gmlp, w1, w2, gfin)


def _swap_halves(t):
    half = t.shape[-1] // 2
    return jnp.concatenate([t[..., half:], t[..., :half]], axis=-1)


def _rope_tables(s):
    inv = 1.0 / (ROPE_THETA ** (jnp.arange(0, QK_ROPE, 2, dtype=F32) / QK_ROPE))
    ang = jnp.arange(s, dtype=F32)[:, None] * inv[None, :]
    cos, sin = jnp.cos(ang), jnp.sin(ang)
    zeros = jnp.zeros((s, 128 - QK_ROPE), F32)
    return (jnp.concatenate([cos, cos, zeros], axis=1),
            jnp.concatenate([-sin, sin, zeros], axis=1))


def _pick(n, pref):
    t = min(n, pref)
    while n % t:
        t //= 2
    return t


def _trunk(x, mod, modf, w):
    b, s, d = x.shape
    cs, sn = _rope_tables(s)
    ts = _pick(s, 256)
    q, k, v, qkv_raw, z, gates = _pre(x, mod, w["gmix"], w["wa"], w["gq"], w["wuq"], w["gkv"], w["wukv"],
                                      cs, sn, ts)
    o_mla = _attention(q, k, v, _pick(s, 512), _pick(s // 2, 512))
    nck = 2 if s % (2 * CHUNK) == 0 else 1
    mw, cm, qt, ol, gm = _gdn_ops(qkv_raw, gates, w["cw8"], w["prm"], nck)
    o_f, o_b = _gdn_scan(mw, cm, qt, ol, gm)
    return _post(x, mod, modf, o_mla, o_f, o_b, z, w["ggdn"], w["wout"], w["gmlp"], w["w1"], w["w2"],
                 w["gfin"], ts)


def kernel(x_prompt, x_sample, c_prompt, c_sample, w_ada, b_ada, g_mix, w_in, g_q, w_uq, g_kv, w_ukv,
           conv_w, a_log_f, a_log_b, dt_f, dt_b, g_gdn, w_out, g_mlp, w_mlp_in, w_mlp_out,
           w_ada_f, b_ada_f, g_final):
    d = D_MODEL
    bp, bs = x_prompt.shape[0], x_sample.shape[0]
    c8 = jnp.zeros((8, d), F32).at[:bp].set(c_prompt).at[bp:bp + bs].set(c_sample)
    mod = _ada(c8, w_ada[0], b_ada[0][None, :])[:, None, :]
    modf = _ada(c8, w_ada_f, b_ada_f[None, :])[:, None, :]

    wi = w_in[0]
    o_kr = Q_LORA + KV_LORA
    o_qkv = o_kr + QK_ROPE
    o_z = o_qkv + GDN_CH
    o_g = o_z + GDN_W
    kr = wi[:, o_kr:o_qkv]
    wa = jnp.concatenate([wi[:, :o_kr], kr, _swap_halves(kr), wi[:, o_qkv:o_g], wi[:, o_g:],
                          jnp.zeros((d, GATE_PAD - 4 * GDN_HEADS), F32)], axis=1).astype(BF16)
    wq = w_uq[0].reshape(Q_LORA, MLA_HEADS, QK_NOPE + QK_ROPE)
    wq_r = wq[..., QK_NOPE:]
    wuq = jnp.concatenate([wq[..., :QK_NOPE], wq_r, _swap_halves(wq_r)], axis=-1)
    wuq = wuq.reshape(Q_LORA, MLA_HEADS * QK_PAD).astype(BF16)
    lanes = jnp.zeros((128,), F32)
    prm = jnp.zeros((8, 128), F32)
    prm = prm.at[0].set(lanes.at[0:4].set(a_log_f[0]).at[4:8].set(a_log_b[0]))
    prm = prm.at[1].set(lanes.at[0:4].set(dt_f[0]).at[4:8].set(dt_b[0]))
    w = dict(
        gmix=g_mix, wa=wa, gq=g_q, wuq=wuq, gkv=g_kv, wukv=w_ukv[0].astype(BF16),
        cw8=jnp.zeros((8, GDN_CH), F32).at[:CONV_W].set(conv_w[0]), prm=prm,
        ggdn=g_gdn, wout=w_out[0].astype(BF16), gmlp=g_mlp,
        w1=w_mlp_in[0].astype(BF16), w2=w_mlp_out[0].astype(BF16), gfin=g_final[None, :],
    )
    y_p = _trunk(x_prompt, mod[:bp], modf[:bp], w)
    y_s = _trunk(x_sample, mod[bp:bp + bs], modf[bp:bp + bs], w)
    return (y_p, y_s)
```

```python
import functools

import jax
import jax.numpy as jnp
from jax import lax
from jax.experimental import pallas as pl
from jax.experimental.pallas import tpu as pltpu

F32 = jnp.float32
BF16 = jnp.bfloat16

EPS = 1e-6
LOG2_E = 1.4426950408889634
D_MODEL = 1024
N_MOD = 6
MLA_HEADS = 4
QK_NOPE = 128
QK_ROPE = 64
V_DIM = 128
Q_LORA = 384
KV_LORA = 256
ROPE_THETA = 10000.0
GDN_HEADS = 4
GDN_DK = 128
GDN_DV = 128
CONV_W = 5
D_FF = 4 * D_MODEL
QK_PAD = 256
VT_ROWS = V_DIM + 16
GDN_CH = 2 * GDN_HEADS * GDN_DK + GDN_HEADS * GDN_DV
GDN_W = GDN_HEADS * GDN_DV
MLA_W = MLA_HEADS * V_DIM
GATE_PAD = 128
C_CQ = 0
C_CKV = C_CQ + Q_LORA
C_KR = C_CKV + KV_LORA
C_QKV = C_KR + 128
C_Z = C_QKV + GDN_CH
C_GATE = C_Z + GDN_W
WA_COLS = C_GATE + GATE_PAD

CHUNK = 128
INV_BLK = 16
VMEM_LIMIT = 56 * 1024 * 1024


def _dot(a, b):
    return jnp.dot(a, b, preferred_element_type=F32)


def _dot_bf(a, b):
    return jnp.dot(a.astype(BF16), b.astype(BF16), preferred_element_type=F32)


def _split2(a):
    hi = a.astype(BF16)
    lo = (a - hi.astype(F32)).astype(BF16)
    return hi, lo


def _dot_x3(a, b):
    ah, al = _split2(a)
    bh, bl = _split2(b)
    return _dot(ah, bh) + (_dot(ah, bl) + _dot(al, bh))


def _sigmoid(x):
    return 1.0 / (1.0 + jnp.exp(-x))


def _silu(x):
    return x * _sigmoid(x)


def _cparams(sem):
    return pltpu.CompilerParams(dimension_semantics=sem, vmem_limit_bytes=VMEM_LIMIT)


def _ada_kernel(c_ref, w_ref, b_ref, o_ref):
    sc = _silu(c_ref[...])
    o_ref[...] = _dot_bf(sc, w_ref[...]) + b_ref[...]


def _ada(c8, w, b):
    d, n = w.shape
    tn = 1024
    return pl.pallas_call(
        _ada_kernel,
        out_shape=jax.ShapeDtypeStruct((8, n), F32),
        grid=(n // tn,),
        in_specs=[pl.BlockSpec((8, d), lambda j: (0, 0)),
                  pl.BlockSpec((d, tn), lambda j: (0, j)),
                  pl.BlockSpec((1, tn), lambda j: (0, j))],
        out_specs=pl.BlockSpec((8, tn), lambda j: (0, j)),
        compiler_params=_cparams(("parallel",)),
        name="ada",
    )(c8, w, b)


def _rms(x, g):
    ms = jnp.mean(x * x, axis=-1, keepdims=True)
    return x * lax.rsqrt(ms + EPS) * g


def _pre_kernel(x_ref, mod_ref, gmix_ref, wa_ref, gq_ref, wuq_ref, gkv_ref, wukv_ref, cs_ref, sn_ref,
                q_ref, k_ref, v_ref, qkv_ref, z_ref, gate_ref):
    d = D_MODEL
    x = x_ref[0]
    mod = mod_ref[0]
    shift = mod[:, 0:d]
    scale = mod[:, d:2 * d]
    hb = (_rms(x, gmix_ref[...]) * (1.0 + scale) + shift).astype(BF16)

    def proj(lo, hi):
        return _dot(hb, wa_ref[:, lo:hi])

    cs = cs_ref[...]
    sn = sn_ref[...]

    def rope(t):
        return t * cs + pltpu.roll(t, 64, 1) * sn

    qk_scale = (QK_NOPE + QK_ROPE) ** -0.5 * LOG2_E
    ones = jnp.ones((VT_ROWS - V_DIM, x.shape[0]), BF16)
    cq = _rms(proj(C_CQ, C_CKV), gq_ref[...])
    q = _dot_bf(cq, wuq_ref[...])
    ckv = _rms(proj(C_CKV, C_KR), gkv_ref[...])
    kv = _dot_bf(ckv, wukv_ref[...])
    kr = rope(proj(C_KR, C_QKV)).astype(BF16)
    for h in range(MLA_HEADS):
        o = h * QK_PAD
        q_ref[0, h, 0:128, :] = (q[:, o:o + 128] * qk_scale).T.astype(BF16)
        q_ref[0, h, 128:256, :] = (rope(q[:, o + 128:o + 256]) * qk_scale).T.astype(BF16)
        k_ref[0, h, :, 0:128] = kv[:, o:o + 128].astype(BF16)
        k_ref[0, h, :, 128:256] = kr
        v_ref[0, h, 0:V_DIM, :] = kv[:, o + 128:o + 256].T.astype(BF16)
        v_ref[0, h, V_DIM:VT_ROWS, :] = ones
    qkv_ref[0] = proj(C_QKV, C_Z)
    z_ref[0] = proj(C_Z, C_GATE)
    gate_ref[0] = proj(C_GATE, WA_COLS)


def _pre(x, mod, gmix, wa, gq, wuq, gkv, wukv, cs, sn, ts):
    b, s, d = x.shape
    h = MLA_HEADS
    const = lambda bi, si: (0, 0)
    return pl.pallas_call(
        _pre_kernel,
        out_shape=(jax.ShapeDtypeStruct((b, h, QK_PAD, s), BF16),
                   jax.ShapeDtypeStruct((b, h, s, QK_PAD), BF16),
                   jax.ShapeDtypeStruct((b, h, VT_ROWS, s), BF16),
                   jax.ShapeDtypeStruct((b, s, GDN_CH), F32),
                   jax.ShapeDtypeStruct((b, s, GDN_W), F32),
                   jax.ShapeDtypeStruct((b, s, GATE_PAD), F32)),
        grid=(b, s // ts),
        in_specs=[pl.BlockSpec((1, ts, d), lambda bi, si: (bi, si, 0)),
                  pl.BlockSpec((1, 1, N_MOD * d), lambda bi, si: (bi, 0, 0)),
                  pl.BlockSpec((1, d), const),
                  pl.BlockSpec((d, WA_COLS), const, pipeline_mode=pl.Buffered(1)),
                  pl.BlockSpec((1, Q_LORA), const),
                  pl.BlockSpec((Q_LORA, h * QK_PAD), const),
                  pl.BlockSpec((1, KV_LORA), const),
                  pl.BlockSpec((KV_LORA, h * 256), const),
                  pl.BlockSpec((ts, 128), lambda bi, si: (si, 0)),
                  pl.BlockSpec((ts, 128), lambda bi, si: (si, 0))],
        out_specs=(pl.BlockSpec((1, h, QK_PAD, ts), lambda bi, si: (bi, 0, 0, si)),
                   pl.BlockSpec((1, h, ts, QK_PAD), lambda bi, si: (bi, 0, si, 0)),
                   pl.BlockSpec((1, h, VT_ROWS, ts), lambda bi, si: (bi, 0, 0, si)),
                   pl.BlockSpec((1, ts, GDN_CH), lambda bi, si: (bi, si, 0)),
                   pl.BlockSpec((1, ts, GDN_W), lambda bi, si: (bi, si, 0)),
                   pl.BlockSpec((1, ts, GATE_PAD), lambda bi, si: (bi, si, 0))),
        compiler_params=_cparams(("parallel", "parallel")),
        name="pre",
    )(x, mod, gmix, wa, gq, wuq, gkv, wukv, cs, sn)


def _attn_kernel(qt_ref, k_ref, vt_ref, o_ref, s_ref, p_ref, acc_ref, m_ref, al_ref, mt_ref, *, tk, nk):
    qt = qt_ref[0, 0]

    def scores(i, slot):
        s = _dot(k_ref[0, 0, pl.ds(pl.multiple_of(i * tk, tk), tk), :], qt)
        s_ref[slot] = s
        mt_ref[slot] = jnp.max(s, axis=0, keepdims=True)

    def flush(i, slot):
        vt = vt_ref[0, 0, :, pl.ds(pl.multiple_of(i * tk, tk), tk)]
        acc_ref[...] = al_ref[slot] * acc_ref[...] + _dot(vt, p_ref[slot])

    def step(i, cur):
        m = m_ref[...]
        m_new = jnp.maximum(m, mt_ref[cur])
        al_ref[cur] = jnp.exp2(m - m_new)
        m_ref[...] = m_new
        p_ref[cur] = jnp.exp2(s_ref[cur] - m_new).astype(BF16)
        flush(jnp.maximum(i - 1, 0), 1 - cur)
        scores(jnp.minimum(i + 1, nk - 1), 1 - cur)

    m_ref[...] = jnp.full(m_ref.shape, -jnp.inf, F32)
    al_ref[...] = jnp.zeros(al_ref.shape, F32)
    acc_ref[...] = jnp.zeros(acc_ref.shape, F32)
    p_ref[1] = jnp.zeros(p_ref.shape[1:], BF16)
    scores(0, 0)

    def pair(j, carry):
        step(2 * j, 0)

        @pl.when(j >= 0)
        def _():
            step(2 * j + 1, 1)

        return carry

    lax.fori_loop(0, nk // 2, pair, 0)
    flush(nk - 1, 1)
    acc = acc_ref[...]
    o_ref[0] = (acc[:V_DIM] / acc[V_DIM:V_DIM + 1]).T.astype(o_ref.dtype)


def _attention(qt, k, vt, tq, tk):
    b, h, s, _ = k.shape
    assert (s // tk) % 2 == 0
    return pl.pallas_call(
        functools.partial(_attn_kernel, tk=tk, nk=s // tk),
        out_shape=jax.ShapeDtypeStruct((b, s, h * V_DIM), BF16),
        grid=(b, h, s // tq),
        in_specs=[pl.BlockSpec((1, 1, QK_PAD, tq), lambda bi, hi, qi: (bi, hi, 0, qi)),
                  pl.BlockSpec((1, 1, s, QK_PAD), lambda bi, hi, qi: (bi, hi, 0, 0)),
                  pl.BlockSpec((1, 1, VT_ROWS, s), lambda bi, hi, qi: (bi, hi, 0, 0))],
        out_specs=pl.BlockSpec((1, tq, V_DIM), lambda bi, hi, qi: (bi, qi, hi)),
        scratch_shapes=[pltpu.VMEM((2, tk, tq), F32), pltpu.VMEM((2, tk, tq), BF16),
                        pltpu.VMEM((VT_ROWS, tq), F32), pltpu.VMEM((1, tq), F32), pltpu.VMEM((2, 1, tq), F32),
                        pltpu.VMEM((2, 1, tq), F32)],
        compiler_params=_cparams(("parallel", "parallel", "parallel")),
        name="attn",
    )(qt, k, vt)


def _bf(x):
    return x.astype(BF16)


def _cat(a, b):
    return jnp.concatenate([a, b], axis=1)


def _tri_solve_many(a_list, rhs_list, row, col):
    c = CHUNK
    eye = (row == col).astype(F32)
    blk = (row // INV_BLK) == (col // INV_BLK)
    n = range(len(a_list))
    x0 = [jnp.where(blk, -a, 0.0) for a in a_list]
    e = [_bf(jnp.where(blk, 0.0, a)) for a in a_list]
    xb = [_bf(x) for x in x0]
    x1 = [_dot(xb[i], xb[i]) for i in n]
    s = [eye + x0[i] for i in n]
    xb = [_bf(x) for x in x1]
    r = [_dot(xb[i], _cat(xb[i], _bf(s[i]))) for i in n]
    s = [s[i] + r[i][:, c:] for i in n]
    xb = [_bf(r[i][:, :c]) for i in n]
    r = [_dot(xb[i], _cat(xb[i], _bf(s[i]))) for i in n]
    s = [s[i] + r[i][:, c:] for i in n]
    xb = [_bf(r[i][:, :c]) for i in n]
    sb = [_bf(s[i]) for i in n]
    s = [s[i] + _dot(xb[i], sb[i]) for i in n]
    sb = [_bf(s[i]) for i in n]
    y0 = [-_dot(sb[i], e[i]) for i in n]
    dr = [_dot(sb[i], _bf(rhs_list[i])) for i in n]
    yb = [_bf(y) for y in y0]
    y1 = [_dot(yb[i], yb[i]) for i in n]
    t = [eye + y0[i] for i in n]
    yb = [_bf(y) for y in y1]
    r = [_dot(yb[i], _cat(yb[i], _bf(t[i]))) for i in n]
    t = [t[i] + r[i][:, c:] for i in n]
    yb = [_bf(r[i][:, :c]) for i in n]
    tb = [_bf(t[i]) for i in n]
    t = [t[i] + _dot(yb[i], tb[i]) for i in n]
    return [_dot(_bf(t[i]), _bf(dr[i])) for i in n]


def _gdn_kernel(prev_ref, main_ref, next_ref, gate_ref, cw_ref, prm_ref,
                mw_ref, cm_ref, qt_ref, ol_ref, gm_ref, xe_ref, *, nck):
    tc = nck * CHUNK
    si = pl.program_id(1)
    ns = pl.num_programs(1)
    hh = GDN_HEADS
    xe_ref[0:8, :] = jnp.where(si > 0, prev_ref[0], 0.0)
    xe_ref[8:8 + tc, :] = main_ref[0]
    xe_ref[8 + tc:16 + tc, :] = jnp.where(si < ns - 1, next_ref[0], 0.0)
    pad = CONV_W // 2
    acc = cw_ref[0:1, :] * xe_ref[8 - pad:8 - pad + tc, :]
    for j in range(1, CONV_W):
        acc = acc + cw_ref[j:j + 1, :] * xe_ref[8 - pad + j:8 - pad + j + tc, :]
    qkv = _silu(acc)
    gs = gate_ref[0]
    a = gs + prm_ref[1:2, :]
    softplus = jnp.maximum(a, 0.0) + jnp.log1p(jnp.exp(-jnp.abs(a)))
    lane = lax.broadcasted_iota(jnp.int32, (tc, 128), 1)
    glog = jnp.where(lane < 2 * hh, -jnp.exp(prm_ref[0:1, :]) * softplus, 0.0)
    beta = _sigmoid(gs)
    g1 = _bf(glog).astype(F32)
    r1 = glog - g1
    g2 = _bf(r1).astype(F32)
    g3 = r1 - g2
    gpk = _bf(g1 + pltpu.roll(g2, 8, 1) + pltpu.roll(g3, 16, 1))

    row = lax.broadcasted_iota(jnp.int32, (CHUNK, CHUNK), 0)
    col = lax.broadcasted_iota(jnp.int32, (CHUNK, CHUNK), 1)
    low_bf = (row >= col).astype(BF16)
    up_bf = (row <= col).astype(BF16)
    masks = ((row >= col, row > col), (row <= col, row < col))

    for c in range(nck):
        r0 = c * CHUNK
        pk = gpk[r0:r0 + CHUNK, :]
        pre = _dot(low_bf, pk)
        suf = _dot(up_bf, pk)
        gsum = jnp.where(col < hh, pre, suf)
        gsum = gsum + pltpu.roll(gsum, 120, 1) + pltpu.roll(gsum, 112, 1)
        gsum_t = gsum.T
        a_list, rhs_list, inst = [], [], []
        for h in range(hh):
            qh = qkv[r0:r0 + CHUNK, h * GDN_DK:(h + 1) * GDN_DK]
            kh = qkv[r0:r0 + CHUNK, (hh + h) * GDN_DK:(hh + h + 1) * GDN_DK]
            vh = qkv[r0:r0 + CHUNK, 2 * hh * GDN_DK + h * GDN_DV:2 * hh * GDN_DK + (h + 1) * GDN_DV]
            qh = qh * (lax.rsqrt(jnp.sum(qh * qh, axis=-1, keepdims=True) + EPS) * (GDN_DK ** -0.5))
            kh = kh * lax.rsqrt(jnp.sum(kh * kh, axis=-1, keepdims=True) + EPS)
            kb = _bf(kh)
            gram = lax.dot_general(jnp.concatenate([kb, _bf(qh)], axis=0), kb,
                                   (((1,), (1,)), ((), ())), preferred_element_type=F32)
            kk = gram[:CHUNK]
            qk = gram[CHUNK:]
            for d in range(2):
                idx = d * hh + h
                incl, strict = masks[d]
                g_c = gsum[:, idx:idx + 1]
                g_r = gsum_t[idx:idx + 1, :]
                decay = jnp.where(incl, jnp.exp(jnp.where(incl, g_c - g_r, 0.0)), 0.0)
                b_c = beta[r0:r0 + CHUNK, 2 * hh + idx:2 * hh + idx + 1]
                eg = jnp.exp(g_c)
                g_tot = g_c[CHUNK - 1:CHUNK, :] if d == 0 else g_c[0:1, :]
                a_list.append(jnp.where(strict, (b_c * kk) * decay, 0.0))
                rhs_list.append(_cat(vh * b_c, kh * (b_c * eg)))
                inst.append((h, d, idx, _bf(jnp.where(incl, qk * decay, 0.0)),
                             _bf((kh * jnp.exp(g_tot - g_c)).T), qh * eg, jnp.exp(g_tot)))
        uw = _tri_solve_many(a_list, rhs_list, row, col)
        uwb = [_bf(x) for x in uw]
        au = [_dot(inst[i][3], uwb[i]) for i in range(len(inst))]
        kt = [_dot(inst[i][4], uwb[i]) for i in range(len(inst))]
        for i, (h, d, idx, _, _, qdec, gam) in enumerate(inst):
            ol_ref[0, c, d, h] = au[i][:, 0:GDN_DV]
            qt_ref[0, c, d, h] = _bf(qdec - au[i][:, GDN_DV:])
            cm_ref[0, c, d, h] = kt[i][:, 0:GDN_DV]
            mw_ref[0, c, d, h] = _bf(kt[i][:, GDN_DV:])
            gm_ref[0, c, idx:idx + 1, :] = jnp.broadcast_to(gam, (1, 128))


def _gdn_ops(qkv_raw, gates, cw8, prm, nck):
    b, s, _ = qkv_raw.shape
    tc = nck * CHUNK
    n = s // CHUNK
    r8 = tc // 8
    nb8 = s // 8
    op_shape = jax.ShapeDtypeStruct((b, n, 2, GDN_HEADS, CHUNK, 128), F32)
    op_bf16 = jax.ShapeDtypeStruct(op_shape.shape, BF16)
    op_spec = pl.BlockSpec((1, nck, 2, GDN_HEADS, CHUNK, 128), lambda bi, si: (bi, si, 0, 0, 0, 0))
    return pl.pallas_call(
        functools.partial(_gdn_kernel, nck=nck),
        out_shape=(op_bf16, op_shape, op_bf16, op_shape,
                   jax.ShapeDtypeStruct((b, n, 8, 128), F32)),
        grid=(b, s // tc),
        in_specs=[pl.BlockSpec((1, 8, GDN_CH), lambda bi, si: (bi, jnp.maximum(si * r8 - 1, 0), 0)),
                  pl.BlockSpec((1, tc, GDN_CH), lambda bi, si: (bi, si, 0)),
                  pl.BlockSpec((1, 8, GDN_CH), lambda bi, si: (bi, jnp.minimum((si + 1) * r8, nb8 - 1), 0)),
                  pl.BlockSpec((1, tc, GATE_PAD), lambda bi, si: (bi, si, 0)),
                  pl.BlockSpec((8, GDN_CH), lambda bi, si: (0, 0)),
                  pl.BlockSpec((8, 128), lambda bi, si: (0, 0))],
        out_specs=(op_spec, op_spec, op_spec, op_spec,
                   pl.BlockSpec((1, nck, 8, 128), lambda bi, si: (bi, si, 0, 0))),
        scratch_shapes=[pltpu.VMEM((tc + 16, GDN_CH), F32)],
        compiler_params=_cparams(("parallel", "parallel")),
        name="gdn_ops",
    )(qkv_raw, qkv_raw, qkv_raw, gates, cw8, prm)


def _scan_kernel(mwf, cmf, qtf, olf, gmf, mwb, cmb, qtb, olb, gmb, of_ref, ob_ref, s_ref):
    @pl.when(pl.program_id(1) == 0)
    def _():
        s_ref[...] = jnp.zeros_like(s_ref)

    hh = GDN_HEADS
    for d, (mw, cm, qt, ol, gm, o_ref) in enumerate(((mwf, cmf, qtf, olf, gmf, of_ref),
                                                     (mwb, cmb, qtb, olb, gmb, ob_ref))):
        for h in range(hh):
            idx = d * hh + h
            st = s_ref[idx]
            sb = st.astype(BF16)
            o_ref[0, :, h * GDN_DV:(h + 1) * GDN_DV] = _dot(qt[0, 0, 0, h], sb) + ol[0, 0, 0, h]
            gam = gm[0, 0, idx:idx + 1, :]
            s_ref[idx] = st * gam - _dot(mw[0, 0, 0, h], sb) + cm[0, 0, 0, h]


def _gdn_scan(mw, cm, qt, ol, gm):
    b, n = mw.shape[0], mw.shape[1]
    s = n * CHUNK
    blk = (1, 1, 1, GDN_HEADS, CHUNK, 128)
    fwd = lambda bi, ni: (bi, ni, 0, 0, 0, 0)
    bwd = lambda bi, ni: (bi, n - 1 - ni, 1, 0, 0, 0)
    gfwd = lambda bi, ni: (bi, ni, 0, 0)
    gbwd = lambda bi, ni: (bi, n - 1 - ni, 0, 0)
    o_shape = jax.ShapeDtypeStruct((b, s, GDN_W), F32)
    return pl.pallas_call(
        _scan_kernel,
        out_shape=(o_shape, o_shape),
        grid=(b, n),
        in_specs=[pl.BlockSpec(blk, fwd)] * 4 + [pl.BlockSpec((1, 1, 8, 128), gfwd)]
        + [pl.BlockSpec(blk, bwd)] * 4 + [pl.BlockSpec((1, 1, 8, 128), gbwd)],
        out_specs=(pl.BlockSpec((1, CHUNK, GDN_W), lambda bi, ni: (bi, ni, 0)),
                   pl.BlockSpec((1, CHUNK, GDN_W), lambda bi, ni: (bi, n - 1 - ni, 0))),
        scratch_shapes=[pltpu.VMEM((2 * GDN_HEADS, GDN_DK, GDN_DV), F32)],
        compiler_params=_cparams(("parallel", "arbitrary")),
        name="gdn_scan",
    )(mw, cm, qt, ol, gm, mw, cm, qt, ol, gm)


def _post_kernel(x_ref, mod_ref, modf_ref, om_ref, of_ref, ob_ref, z_ref, ggdn_ref, wout_ref,
                 gmlp_ref, w1_ref, w2_ref, gfin_ref, y_ref, *, ff_blk):
    d = D_MODEL
    x = x_ref[0]
    mod = mod_ref[0]
    gate_a = mod[:, 2 * d:3 * d]
    shift_m = mod[:, 3 * d:4 * d]
    scale_m = mod[:, 4 * d:5 * d]
    gate_m = mod[:, 5 * d:6 * d]
    modf = modf_ref[0]
    shift_f = modf[:, 0:d]
    scale_f = modf[:, d:2 * d]

    o = of_ref[0] + ob_ref[0]
    z = z_ref[0]
    mixed = _dot(om_ref[0], wout_ref[0:MLA_W, :])
    for h in range(GDN_HEADS):
        sl = slice(h * GDN_DV, (h + 1) * GDN_DV)
        og = _rms(o[:, sl], ggdn_ref[...]) * _silu(z[:, sl])
        mixed = mixed + _dot(og.astype(BF16), wout_ref[MLA_W + h * GDN_DV:MLA_W + (h + 1) * GDN_DV, :])
    x1 = x + gate_a * mixed
    hb = (_rms(x1, gmlp_ref[...]) * (1.0 + scale_m) + shift_m).astype(BF16)
    acc = jnp.zeros_like(x1)
    for j in range(D_FF // ff_blk):
        hid = _dot(hb, w1_ref[:, j * ff_blk:(j + 1) * ff_blk])
        hid = jnp.square(jnp.maximum(hid, 0.0)).astype(BF16)
        acc = acc + _dot(hid, w2_ref[j * ff_blk:(j + 1) * ff_blk, :])
    x2 = x1 + gate_m * acc
    y_ref[0] = _rms(x2, gfin_ref[...]) * (1.0 + scale_f) + shift_f


def _post(x, mod, modf, om, of, ob, z, ggdn, wout, gmlp, w1, w2, gfin, ts):
    b, s, d = x.shape
    const = lambda bi, si: (0, 0)
    tok = lambda w: pl.BlockSpec((1, ts, w), lambda bi, si: (bi, si, 0))
    one = pl.Buffered(1)
    return pl.pallas_call(
        functools.partial(_post_kernel, ff_blk=1024),
        out_shape=jax.ShapeDtypeStruct((b, s, d), F32),
        grid=(b, s // ts),
        in_specs=[tok(d),
                  pl.BlockSpec((1, 1, N_MOD * d), lambda bi, si: (bi, 0, 0)),
                  pl.BlockSpec((1, 1, 2 * d), lambda bi, si: (bi, 0, 0)),
                  tok(MLA_W), tok(GDN_W), tok(GDN_W), tok(GDN_W),
                  pl.BlockSpec((1, GDN_DV), const),
                  pl.BlockSpec((MLA_W + GDN_W, d), const, pipeline_mode=one),
                  pl.BlockSpec((1, d), const),
                  pl.BlockSpec((d, D_FF), const, pipeline_mode=one),
                  pl.BlockSpec((D_FF, d), const, pipeline_mode=one),
                  pl.BlockSpec((1, d), const)],
        out_specs=tok(d),
        compiler_params=_cparams(("parallel", "parallel")),
        name="post",
    )(x, mod, modf, om, of, ob, z, ggdn, wout, gmlp, w1, w2, gfin)


def _swap_halves(t):
    half = t.shape[-1] // 2
    return jnp.concatenate([t[..., half:], t[..., :half]], axis=-1)


def _rope_tables(s):
    inv = 1.0 / (ROPE_THETA ** (jnp.arange(0, QK_ROPE, 2, dtype=F32) / QK_ROPE))
    ang = jnp.arange(s, dtype=F32)[:, None] * inv[None, :]
    cos, sin = jnp.cos(ang), jnp.sin(ang)
    zeros = jnp.zeros((s, 128 - QK_ROPE), F32)
    return (jnp.concatenate([cos, cos, zeros], axis=1),
            jnp.concatenate([-sin, sin, zeros], axis=1))


def _pick(n, pref):
    t = min(n, pref)
    while n % t:
        t //= 2
    return t


def _trunk(x, mod, modf, w):
    b, s, d = x.shape
    cs, sn = _rope_tables(s)
    ts = _pick(s, 512)
    q, k, v, qkv_raw, z, gates = _pre(x, mod, w["gmix"], w["wa"], w["gq"], w["wuq"], w["gkv"], w["wukv"],
                                      cs, sn, ts)
    o_mla = _attention(q, k, v, _pick(s, 512), _pick(s // 2, 1024))
    nck = 2 if s % (2 * CHUNK) == 0 else 1
    mw, cm, qt, ol, gm = _gdn_ops(qkv_raw, gates, w["cw8"], w["prm"], nck)
    o_f, o_b = _gdn_scan(mw, cm, qt, ol, gm)
    return _post(x, mod, modf, o_mla, o_f, o_b, z, w["ggdn"], w["wout"], w["gmlp"], w["w1"], w["w2"],
                 w["gfin"], ts)


def kernel(x_prompt, x_sample, c_prompt, c_sample, w_ada, b_ada, g_mix, w_in, g_q, w_uq, g_kv, w_ukv,
           conv_w, a_log_f, a_log_b, dt_f, dt_b, g_gdn, w_out, g_mlp, w_mlp_in, w_mlp_out,
           w_ada_f, b_ada_f, g_final):
    d = D_MODEL
    bp, bs = x_prompt.shape[0], x_sample.shape[0]
    c8 = jnp.zeros((8, d), F32).at[:bp].set(c_prompt).at[bp:bp + bs].set(c_sample)
    mod = _ada(c8, w_ada[0], b_ada[0][None, :])[:, None, :]
    modf = _ada(c8, w_ada_f, b_ada_f[None, :])[:, None, :]

    wi = w_in[0]
    o_kr = Q_LORA + KV_LORA
    o_qkv = o_kr + QK_ROPE
    o_z = o_qkv + GDN_CH
    o_g = o_z + GDN_W
    kr = wi[:, o_kr:o_qkv]
    wa = jnp.concatenate([wi[:, :o_kr], kr, _swap_halves(kr), wi[:, o_qkv:o_g], wi[:, o_g:],
                          jnp.zeros((d, GATE_PAD - 4 * GDN_HEADS), F32)], axis=1).astype(BF16)
    wq = w_uq[0].reshape(Q_LORA, MLA_HEADS, QK_NOPE + QK_ROPE)
    wq_r = wq[..., QK_NOPE:]
    wuq = jnp.concatenate([wq[..., :QK_NOPE], wq_r, _swap_halves(wq_r)], axis=-1)
    wuq = wuq.reshape(Q_LORA, MLA_HEADS * QK_PAD).astype(BF16)
    lanes = jnp.zeros((128,), F32)
    prm = jnp.zeros((8, 128), F32)
    prm = prm.at[0].set(lanes.at[0:4].set(a_log_f[0]).at[4:8].set(a_log_b[0]))
    prm = prm.at[1].set(lanes.at[0:4].set(dt_f[0]).at[4:8].set(dt_b[0]))
    w = dict(
        gmix=g_mix, wa=wa, gq=g_q, wuq=wuq, gkv=g_kv, wukv=w_ukv[0].astype(BF16),
        cw8=jnp.zeros((8, GDN_CH), F32).at[:CONV_W].set(conv_w[0]), prm=prm,
        ggdn=g_gdn, wout=w_out[0].astype(BF16), gmlp=g_mlp,
        w1=w_mlp_in[0].astype(BF16), w2=w_mlp_out[0].astype(BF16), gfin=g_final[None, :],
    )
    y_p = _trunk(x_prompt, mod[:bp], modf[:bp], w)
    y_s = _trunk(x_sample, mod[bp:bp + bs], modf[bp:bp + bs], w)
    return (y_p, y_s)
```

```python
import functools

import jax
import jax.numpy as jnp
from jax import lax
from jax.experimental import pallas as pl
from jax.experimental.pallas import tpu as pltpu

F32 = jnp.float32
BF16 = jnp.bfloat16

EPS = 1e-6
LOG2_E = 1.4426950408889634
D_MODEL = 1024
N_MOD = 6
MLA_HEADS = 4
QK_NOPE = 128
QK_ROPE = 64
V_DIM = 128
Q_LORA = 384
KV_LORA = 256
ROPE_THETA = 10000.0
GDN_HEADS = 4
GDN_DK = 128
GDN_DV = 128
CONV_W = 5
D_FF = 4 * D_MODEL
QK_PAD = 256
VT_ROWS = V_DIM + 16
GDN_CH = 2 * GDN_HEADS * GDN_DK + GDN_HEADS * GDN_DV
GDN_W = GDN_HEADS * GDN_DV
MLA_W = MLA_HEADS * V_DIM
GATE_PAD = 128
C_CQ = 0
C_CKV = C_CQ + Q_LORA
C_KR = C_CKV + KV_LORA
C_QKV = C_KR + 128
C_Z = C_QKV + GDN_CH
C_GATE = C_Z + GDN_W
WA_COLS = C_GATE + GATE_PAD

CHUNK = 128
INV_BLK = 16
VMEM_LIMIT = 56 * 1024 * 1024


def _dot(a, b):
    return jnp.dot(a, b, preferred_element_type=F32)


def _dot_bf(a, b):
    return jnp.dot(a.astype(BF16), b.astype(BF16), preferred_element_type=F32)


def _split2(a):
    hi = a.astype(BF16)
    lo = (a - hi.astype(F32)).astype(BF16)
    return hi, lo


def _dot_x3(a, b):
    ah, al = _split2(a)
    bh, bl = _split2(b)
    return _dot(ah, bh) + (_dot(ah, bl) + _dot(al, bh))


def _sigmoid(x):
    return 1.0 / (1.0 + jnp.exp(-x))


def _silu(x):
    return x * _sigmoid(x)


def _cparams(sem):
    return pltpu.CompilerParams(dimension_semantics=sem, vmem_limit_bytes=VMEM_LIMIT)


def _ada_kernel(c_ref, w_ref, b_ref, o_ref):
    sc = _silu(c_ref[...])
    o_ref[...] = _dot_bf(sc, w_ref[...]) + b_ref[...]


def _ada(c8, w, b):
    d, n = w.shape
    tn = 1024
    return pl.pallas_call(
        _ada_kernel,
        out_shape=jax.ShapeDtypeStruct((8, n), F32),
        grid=(n // tn,),
        in_specs=[pl.BlockSpec((8, d), lambda j: (0, 0)),
                  pl.BlockSpec((d, tn), lambda j: (0, j)),
                  pl.BlockSpec((1, tn), lambda j: (0, j))],
        out_specs=pl.BlockSpec((8, tn), lambda j: (0, j)),
        compiler_params=_cparams(("parallel",)),
        name="ada",
    )(c8, w, b)


def _rms(x, g):
    ms = jnp.mean(x * x, axis=-1, keepdims=True)
    return x * lax.rsqrt(ms + EPS) * g


def _pre_kernel(x_ref, mod_ref, gmix_ref, wa_ref, gq_ref, wuq_ref, gkv_ref, wukv_ref, cs_ref, sn_ref,
                q_ref, k_ref, v_ref, qkv_ref, z_ref, gate_ref):
    d = D_MODEL
    x = x_ref[0]
    mod = mod_ref[0]
    shift = mod[:, 0:d]
    scale = mod[:, d:2 * d]
    hb = (_rms(x, gmix_ref[...]) * (1.0 + scale) + shift).astype(BF16)

    def proj(lo, hi):
        return _dot(hb, wa_ref[:, lo:hi])

    cs = cs_ref[...]
    sn = sn_ref[...]

    def rope(t):
        return t * cs + pltpu.roll(t, 64, 1) * sn

    qk_scale = (QK_NOPE + QK_ROPE) ** -0.5 * LOG2_E
    ones = jnp.ones((VT_ROWS - V_DIM, x.shape[0]), BF16)
    cq = _rms(proj(C_CQ, C_CKV), gq_ref[...])
    q = _dot_bf(cq, wuq_ref[...])
    ckv = _rms(proj(C_CKV, C_KR), gkv_ref[...])
    kv = _dot_bf(ckv, wukv_ref[...])
    kr = rope(proj(C_KR, C_QKV)).astype(BF16)
    for h in range(MLA_HEADS):
        o = h * QK_PAD
        q_ref[0, h, 0:128, :] = (q[:, o:o + 128] * qk_scale).T.astype(BF16)
        q_ref[0, h, 128:256, :] = (rope(q[:, o + 128:o + 256]) * qk_scale).T.astype(BF16)
        k_ref[0, h, :, 0:128] = kv[:, o:o + 128].astype(BF16)
        k_ref[0, h, :, 128:256] = kr
        v_ref[0, h, 0:V_DIM, :] = kv[:, o + 128:o + 256].T.astype(BF16)
        v_ref[0, h, V_DIM:VT_ROWS, :] = ones
    qkv_ref[0] = proj(C_QKV, C_Z)
    z_ref[0] = proj(C_Z, C_GATE)
    gate_ref[0] = proj(C_GATE, WA_COLS)


def _pre(x, mod, gmix, wa, gq, wuq, gkv, wukv, cs, sn, ts):
    b, s, d = x.shape
    h = MLA_HEADS
    const = lambda bi, si: (0, 0)
    return pl.pallas_call(
        _pre_kernel,
        out_shape=(jax.ShapeDtypeStruct((b, h, QK_PAD, s), BF16),
                   jax.ShapeDtypeStruct((b, h, s, QK_PAD), BF16),
                   jax.ShapeDtypeStruct((b, h, VT_ROWS, s), BF16),
                   jax.ShapeDtypeStruct((b, s, GDN_CH), F32),
                   jax.ShapeDtypeStruct((b, s, GDN_W), F32),
                   jax.ShapeDtypeStruct((b, s, GATE_PAD), F32)),
        grid=(b, s // ts),
        in_specs=[pl.BlockSpec((1, ts, d), lambda bi, si: (bi, si, 0)),
                  pl.BlockSpec((1, 1, N_MOD * d), lambda bi, si: (bi, 0, 0)),
                  pl.BlockSpec((1, d), const),
                  pl.BlockSpec((d, WA_COLS), const, pipeline_mode=pl.Buffered(1)),
                  pl.BlockSpec((1, Q_LORA), const),
                  pl.BlockSpec((Q_LORA, h * QK_PAD), const),
                  pl.BlockSpec((1, KV_LORA), const),
                  pl.BlockSpec((KV_LORA, h * 256), const),
                  pl.BlockSpec((ts, 128), lambda bi, si: (si, 0)),
                  pl.BlockSpec((ts, 128), lambda bi, si: (si, 0))],
        out_specs=(pl.BlockSpec((1, h, QK_PAD, ts), lambda bi, si: (bi, 0, 0, si)),
                   pl.BlockSpec((1, h, ts, QK_PAD), lambda bi, si: (bi, 0, si, 0)),
                   pl.BlockSpec((1, h, VT_ROWS, ts), lambda bi, si: (bi, 0, 0, si)),
                   pl.BlockSpec((1, ts, GDN_CH), lambda bi, si: (bi, si, 0)),
                   pl.BlockSpec((1, ts, GDN_W), lambda bi, si: (bi, si, 0)),
                   pl.BlockSpec((1, ts, GATE_PAD), lambda bi, si: (bi, si, 0))),
        compiler_params=_cparams(("parallel", "parallel")),
        name="pre",
    )(x, mod, gmix, wa, gq, wuq, gkv, wukv, cs, sn)


def _attn_kernel(qt_ref, k_ref, vt_ref, o_ref, s_ref, p_ref, acc_ref, m_ref, al_ref, mt_ref, *, tk, nk):
    qt = qt_ref[0, 0]

    def scores(i, slot):
        s = _dot(k_ref[0, 0, pl.ds(pl.multiple_of(i * tk, tk), tk), :], qt)
        s_ref[slot] = s
        mt_ref[slot] = jnp.max(s, axis=0, keepdims=True)

    def flush(i, slot):
        vt = vt_ref[0, 0, :, pl.ds(pl.multiple_of(i * tk, tk), tk)]
        acc_ref[...] = al_ref[slot] * acc_ref[...] + _dot(vt, p_ref[slot])

    def softmax(cur):
        m = m_ref[...]
        m_new = jnp.maximum(m, mt_ref[cur])
        al_ref[cur] = jnp.exp2(m - m_new)
        m_ref[...] = m_new
        p_ref[cur] = jnp.exp2(s_ref[cur] - m_new).astype(BF16)

    def step(i, cur):
        softmax(cur)
        flush(i - 1, 1 - cur)
        scores(i + 1, 1 - cur)

    m_ref[...] = jnp.full(m_ref.shape, -jnp.inf, F32)
    acc_ref[...] = jnp.zeros(acc_ref.shape, F32)
    scores(0, 0)
    softmax(0)
    scores(1, 1)

    def pair(j, carry):
        step(2 * j + 1, 1)

        @pl.when(j >= 0)
        def _():
            step(2 * j + 2, 0)

        return carry

    lax.fori_loop(0, nk // 2 - 1, pair, 0)
    softmax(1)
    flush(nk - 2, 0)
    flush(nk - 1, 1)
    acc = acc_ref[...]
    o_ref[0] = (acc[:V_DIM] / acc[V_DIM:V_DIM + 1]).T.astype(o_ref.dtype)


def _attention(qt, k, vt, tq, tk):
    b, h, s, _ = k.shape
    assert (s // tk) % 2 == 0
    return pl.pallas_call(
        functools.partial(_attn_kernel, tk=tk, nk=s // tk),
        out_shape=jax.ShapeDtypeStruct((b, s, h * V_DIM), BF16),
        grid=(b, h, s // tq),
        in_specs=[pl.BlockSpec((1, 1, QK_PAD, tq), lambda bi, hi, qi: (bi, hi, 0, qi)),
                  pl.BlockSpec((1, 1, s, QK_PAD), lambda bi, hi, qi: (bi, hi, 0, 0)),
                  pl.BlockSpec((1, 1, VT_ROWS, s), lambda bi, hi, qi: (bi, hi, 0, 0))],
        out_specs=pl.BlockSpec((1, tq, V_DIM), lambda bi, hi, qi: (bi, qi, hi)),
        scratch_shapes=[pltpu.VMEM((2, tk, tq), F32), pltpu.VMEM((2, tk, tq), BF16),
                        pltpu.VMEM((VT_ROWS, tq), F32), pltpu.VMEM((1, tq), F32), pltpu.VMEM((2, 1, tq), F32),
                        pltpu.VMEM((2, 1, tq), F32)],
        compiler_params=_cparams(("parallel", "parallel", "parallel")),
        name="attn",
    )(qt, k, vt)


def _bf(x):
    return x.astype(BF16)


def _cat(a, b):
    return jnp.concatenate([a, b], axis=1)


def _tri_solve_many(a_list, rhs_list, row, col):
    c = CHUNK
    eye = (row == col).astype(F32)
    blk = (row // INV_BLK) == (col // INV_BLK)
    n = range(len(a_list))
    x0 = [jnp.where(blk, -a, 0.0) for a in a_list]
    e = [_bf(jnp.where(blk, 0.0, a)) for a in a_list]
    xb = [_bf(x) for x in x0]
    x1 = [_dot(xb[i], xb[i]) for i in n]
    s = [eye + x0[i] for i in n]
    xb = [_bf(x) for x in x1]
    r = [_dot(xb[i], _cat(xb[i], _bf(s[i]))) for i in n]
    s = [s[i] + r[i][:, c:] for i in n]
    xb = [_bf(r[i][:, :c]) for i in n]
    r = [_dot(xb[i], _cat(xb[i], _bf(s[i]))) for i in n]
    s = [s[i] + r[i][:, c:] for i in n]
    xb = [_bf(r[i][:, :c]) for i in n]
    sb = [_bf(s[i]) for i in n]
    s = [s[i] + _dot(xb[i], sb[i]) for i in n]
    sb = [_bf(s[i]) for i in n]
    y0 = [-_dot(sb[i], e[i]) for i in n]
    dr = [_dot(sb[i], _bf(rhs_list[i])) for i in n]
    yb = [_bf(y) for y in y0]
    y1 = [_dot(yb[i], yb[i]) for i in n]
    t = [eye + y0[i] for i in n]
    yb = [_bf(y) for y in y1]
    r = [_dot(yb[i], _cat(yb[i], _bf(t[i]))) for i in n]
    t = [t[i] + r[i][:, c:] for i in n]
    yb = [_bf(r[i][:, :c]) for i in n]
    tb = [_bf(t[i]) for i in n]
    t = [t[i] + _dot(yb[i], tb[i]) for i in n]
    return [_dot(_bf(t[i]), _bf(dr[i])) for i in n]


def _gdn_kernel(prev_ref, main_ref, next_ref, gate_ref, cw_ref, prm_ref,
                mw_ref, cm_ref, qt_ref, ol_ref, gm_ref, xe_ref, *, nck):
    tc = nck * CHUNK
    si = pl.program_id(1)
    ns = pl.num_programs(1)
    hh = GDN_HEADS
    xe_ref[0:8, :] = jnp.where(si > 0, prev_ref[0], 0.0)
    xe_ref[8:8 + tc, :] = main_ref[0]
    xe_ref[8 + tc:16 + tc, :] = jnp.where(si < ns - 1, next_ref[0], 0.0)
    pad = CONV_W // 2
    acc = cw_ref[0:1, :] * xe_ref[8 - pad:8 - pad + tc, :]
    for j in range(1, CONV_W):
        acc = acc + cw_ref[j:j + 1, :] * xe_ref[8 - pad + j:8 - pad + j + tc, :]
    qkv = _silu(acc)
    gs = gate_ref[0]
    a = gs + prm_ref[1:2, :]
    softplus = jnp.maximum(a, 0.0) + jnp.log1p(jnp.exp(-jnp.abs(a)))
    lane = lax.broadcasted_iota(jnp.int32, (tc, 128), 1)
    glog = jnp.where(lane < 2 * hh, -jnp.exp(prm_ref[0:1, :]) * softplus, 0.0)
    beta = _sigmoid(gs)
    g1 = _bf(glog).astype(F32)
    r1 = glog - g1
    g2 = _bf(r1).astype(F32)
    g3 = r1 - g2
    gpk = _bf(g1 + pltpu.roll(g2, 8, 1) + pltpu.roll(g3, 16, 1))

    row = lax.broadcasted_iota(jnp.int32, (CHUNK, CHUNK), 0)
    col = lax.broadcasted_iota(jnp.int32, (CHUNK, CHUNK), 1)
    low_bf = (row >= col).astype(BF16)
    up_bf = (row <= col).astype(BF16)
    masks = ((row >= col, row > col), (row <= col, row < col))

    for c in range(nck):
        r0 = c * CHUNK
        pk = gpk[r0:r0 + CHUNK, :]
        pre = _dot(low_bf, pk)
        suf = _dot(up_bf, pk)
        gsum = jnp.where(col < hh, pre, suf)
        gsum = gsum + pltpu.roll(gsum, 120, 1) + pltpu.roll(gsum, 112, 1)
        gsum_t = gsum.T
        a_list, rhs_list, inst = [], [], []
        for h in range(hh):
            qh = qkv[r0:r0 + CHUNK, h * GDN_DK:(h + 1) * GDN_DK]
            kh = qkv[r0:r0 + CHUNK, (hh + h) * GDN_DK:(hh + h + 1) * GDN_DK]
            vh = qkv[r0:r0 + CHUNK, 2 * hh * GDN_DK + h * GDN_DV:2 * hh * GDN_DK + (h + 1) * GDN_DV]
            qh = qh * (lax.rsqrt(jnp.sum(qh * qh, axis=-1, keepdims=True) + EPS) * (GDN_DK ** -0.5))
            kh = kh * lax.rsqrt(jnp.sum(kh * kh, axis=-1, keepdims=True) + EPS)
            kb = _bf(kh)
            gram = lax.dot_general(jnp.concatenate([kb, _bf(qh)], axis=0), kb,
                                   (((1,), (1,)), ((), ())), preferred_element_type=F32)
            kk = gram[:CHUNK]
            qk = gram[CHUNK:]
            for d in range(2):
                idx = d * hh + h
                incl, strict = masks[d]
                g_c = gsum[:, idx:idx + 1]
                g_r = gsum_t[idx:idx + 1, :]
                decay = jnp.where(incl, jnp.exp(jnp.where(incl, g_c - g_r, 0.0)), 0.0)
                b_c = beta[r0:r0 + CHUNK, 2 * hh + idx:2 * hh + idx + 1]
                eg = jnp.exp(g_c)
                g_tot = g_c[CHUNK - 1:CHUNK, :] if d == 0 else g_c[0:1, :]
                a_list.append(jnp.where(strict, (b_c * kk) * decay, 0.0))
                rhs_list.append(_cat(vh * b_c, kh * (b_c * eg)))
                inst.append((h, d, idx, _bf(jnp.where(incl, qk * decay, 0.0)),
                             _bf((kh * jnp.exp(g_tot - g_c)).T), qh * eg, jnp.exp(g_tot)))
        uw = _tri_solve_many(a_list, rhs_list, row, col)
        uwb = [_bf(x) for x in uw]
        au = [_dot(inst[i][3], uwb[i]) for i in range(len(inst))]
        kt = [_dot(inst[i][4], uwb[i]) for i in range(len(inst))]
        for i, (h, d, idx, _, _, qdec, gam) in enumerate(inst):
            ol_ref[0, c, d, h] = au[i][:, 0:GDN_DV]
            qt_ref[0, c, d, h] = _bf(qdec - au[i][:, GDN_DV:])
            cm_ref[0, c, d, h] = kt[i][:, 0:GDN_DV]
            mw_ref[0, c, d, h] = _bf(kt[i][:, GDN_DV:])
            gm_ref[0, c, idx:idx + 1, :] = jnp.broadcast_to(gam, (1, 128))


def _gdn_ops(qkv_raw, gates, cw8, prm, nck):
    b, s, _ = qkv_raw.shape
    tc = nck * CHUNK
    n = s // CHUNK
    r8 = tc // 8
    nb8 = s // 8
    op_shape = jax.ShapeDtypeStruct((b, n, 2, GDN_HEADS, CHUNK, 128), F32)
    op_bf16 = jax.ShapeDtypeStruct(op_shape.shape, BF16)
    op_spec = pl.BlockSpec((1, nck, 2, GDN_HEADS, CHUNK, 128), lambda bi, si: (bi, si, 0, 0, 0, 0))
    return pl.pallas_call(
        functools.partial(_gdn_kernel, nck=nck),
        out_shape=(op_bf16, op_shape, op_bf16, op_shape,
                   jax.ShapeDtypeStruct((b, n, 8, 128), F32)),
        grid=(b, s // tc),
        in_specs=[pl.BlockSpec((1, 8, GDN_CH), lambda bi, si: (bi, jnp.maximum(si * r8 - 1, 0), 0)),
                  pl.BlockSpec((1, tc, GDN_CH), lambda bi, si: (bi, si, 0)),
                  pl.BlockSpec((1, 8, GDN_CH), lambda bi, si: (bi, jnp.minimum((si + 1) * r8, nb8 - 1), 0)),
                  pl.BlockSpec((1, tc, GATE_PAD), lambda bi, si: (bi, si, 0)),
                  pl.BlockSpec((8, GDN_CH), lambda bi, si: (0, 0)),
                  pl.BlockSpec((8, 128), lambda bi, si: (0, 0))],
        out_specs=(op_spec, op_spec, op_spec, op_spec,
                   pl.BlockSpec((1, nck, 8, 128), lambda bi, si: (bi, si, 0, 0))),
        scratch_shapes=[pltpu.VMEM((tc + 16, GDN_CH), F32)],
        compiler_params=_cparams(("parallel", "parallel")),
        name="gdn_ops",
    )(qkv_raw, qkv_raw, qkv_raw, gates, cw8, prm)


def _scan_kernel(mwf, cmf, qtf, olf, gmf, mwb, cmb, qtb, olb, gmb, of_ref, ob_ref, s_ref, *, ks):
    @pl.when(pl.program_id(1) == 0)
    def _():
        s_ref[...] = jnp.zeros_like(s_ref)

    hh = GDN_HEADS
    for k in range(ks):
        for d, (mw, cm, qt, ol, gm, o_ref, kk) in enumerate(((mwf, cmf, qtf, olf, gmf, of_ref, k),
                                                             (mwb, cmb, qtb, olb, gmb, ob_ref, ks - 1 - k))):
            for h in range(hh):
                idx = d * hh + h
                st = s_ref[idx]
                sb = st.astype(BF16)
                o_ref[0, kk * CHUNK:(kk + 1) * CHUNK, h * GDN_DV:(h + 1) * GDN_DV] = (
                    _dot(qt[0, kk, 0, h], sb) + ol[0, kk, 0, h])
                gam = gm[0, kk, idx:idx + 1, :]
                s_ref[idx] = st * gam - _dot(mw[0, kk, 0, h], sb) + cm[0, kk, 0, h]


def _gdn_scan(mw, cm, qt, ol, gm, ks):
    b, n = mw.shape[0], mw.shape[1]
    s = n * CHUNK
    nb = n // ks
    blk = (1, ks, 1, GDN_HEADS, CHUNK, 128)
    fwd = lambda bi, ni: (bi, ni, 0, 0, 0, 0)
    bwd = lambda bi, ni: (bi, nb - 1 - ni, 1, 0, 0, 0)
    gfwd = lambda bi, ni: (bi, ni, 0, 0)
    gbwd = lambda bi, ni: (bi, nb - 1 - ni, 0, 0)
    o_shape = jax.ShapeDtypeStruct((b, s, GDN_W), F32)
    return pl.pallas_call(
        functools.partial(_scan_kernel, ks=ks),
        out_shape=(o_shape, o_shape),
        grid=(b, nb),
        in_specs=[pl.BlockSpec(blk, fwd)] * 4 + [pl.BlockSpec((1, ks, 8, 128), gfwd)]
        + [pl.BlockSpec(blk, bwd)] * 4 + [pl.BlockSpec((1, ks, 8, 128), gbwd)],
        out_specs=(pl.BlockSpec((1, ks * CHUNK, GDN_W), lambda bi, ni: (bi, ni, 0)),
                   pl.BlockSpec((1, ks * CHUNK, GDN_W), lambda bi, ni: (bi, nb - 1 - ni, 0))),
        scratch_shapes=[pltpu.VMEM((2 * GDN_HEADS, GDN_DK, GDN_DV), F32)],
        compiler_params=_cparams(("parallel", "arbitrary")),
        name="gdn_scan",
    )(mw, cm, qt, ol, gm, mw, cm, qt, ol, gm)


def _post_kernel(x_ref, mod_ref, modf_ref, om_ref, of_ref, ob_ref, z_ref, ggdn_ref, wout_ref,
                 gmlp_ref, w1_ref, w2_ref, gfin_ref, y_ref, *, ff_blk):
    d = D_MODEL
    x = x_ref[0]
    mod = mod_ref[0]
    gate_a = mod[:, 2 * d:3 * d]
    shift_m = mod[:, 3 * d:4 * d]
    scale_m = mod[:, 4 * d:5 * d]
    gate_m = mod[:, 5 * d:6 * d]
    modf = modf_ref[0]
    shift_f = modf[:, 0:d]
    scale_f = modf[:, d:2 * d]

    o = of_ref[0] + ob_ref[0]
    z = z_ref[0]
    mixed = _dot(om_ref[0], wout_ref[0:MLA_W, :])
    for h in range(GDN_HEADS):
        sl = slice(h * GDN_DV, (h + 1) * GDN_DV)
        og = _rms(o[:, sl], ggdn_ref[...]) * _silu(z[:, sl])
        mixed = mixed + _dot(og.astype(BF16), wout_ref[MLA_W + h * GDN_DV:MLA_W + (h + 1) * GDN_DV, :])
    x1 = x + gate_a * mixed
    hb = (_rms(x1, gmlp_ref[...]) * (1.0 + scale_m) + shift_m).astype(BF16)
    acc = jnp.zeros_like(x1)
    for j in range(D_FF // ff_blk):
        hid = _dot(hb, w1_ref[:, j * ff_blk:(j + 1) * ff_blk])
        hid = jnp.square(jnp.maximum(hid, 0.0)).astype(BF16)
        acc = acc + _dot(hid, w2_ref[j * ff_blk:(j + 1) * ff_blk, :])
    x2 = x1 + gate_m * acc
    y_ref[0] = _rms(x2, gfin_ref[...]) * (1.0 + scale_f) + shift_f


def _post(x, mod, modf, om, of, ob, z, ggdn, wout, gmlp, w1, w2, gfin, ts):
    b, s, d = x.shape
    const = lambda bi, si: (0, 0)
    tok = lambda w: pl.BlockSpec((1, ts, w), lambda bi, si: (bi, si, 0))
    one = pl.Buffered(1)
    return pl.pallas_call(
        functools.partial(_post_kernel, ff_blk=1024),
        out_shape=jax.ShapeDtypeStruct((b, s, d), F32),
        grid=(b, s // ts),
        in_specs=[tok(d),
                  pl.BlockSpec((1, 1, N_MOD * d), lambda bi, si: (bi, 0, 0)),
                  pl.BlockSpec((1, 1, 2 * d), lambda bi, si: (bi, 0, 0)),
                  tok(MLA_W), tok(GDN_W), tok(GDN_W), tok(GDN_W),
                  pl.BlockSpec((1, GDN_DV), const),
                  pl.BlockSpec((MLA_W + GDN_W, d), const, pipeline_mode=one),
                  pl.BlockSpec((1, d), const),
                  pl.BlockSpec((d, D_FF), const, pipeline_mode=one),
                  pl.BlockSpec((D_FF, d), const, pipeline_mode=one),
                  pl.BlockSpec((1, d), const)],
        out_specs=tok(d),
        compiler_params=_cparams(("parallel", "parallel")),
        name="post",
    )(x, mod, modf, om, of, ob, z, ggdn, wout, gmlp, w1, w2, gfin)


def _swap_halves(t):
    half = t.shape[-1] // 2
    return jnp.concatenate([t[..., half:], t[..., :half]], axis=-1)


def _rope_tables(s):
    inv = 1.0 / (ROPE_THETA ** (jnp.arange(0, QK_ROPE, 2, dtype=F32) / QK_ROPE))
    ang = jnp.arange(s, dtype=F32)[:, None] * inv[None, :]
    cos, sin = jnp.cos(ang), jnp.sin(ang)
    zeros = jnp.zeros((s, 128 - QK_ROPE), F32)
    return (jnp.concatenate([cos, cos, zeros], axis=1),
            jnp.concatenate([-sin, sin, zeros], axis=1))


def _pick(n, pref):
    t = min(n, pref)
    while n % t:
        t //= 2
    return t


def _trunk(x, mod, modf, w):
    b, s, d = x.shape
    cs, sn = _rope_tables(s)
    ts = _pick(s, 512)
    q, k, v, qkv_raw, z, gates = _pre(x, mod, w["gmix"], w["wa"], w["gq"], w["wuq"], w["gkv"], w["wukv"],
                                      cs, sn, ts)
    o_mla = _attention(q, k, v, _pick(s, 512), _pick(s // 2, 1024))
    nck = 2 if s % (2 * CHUNK) == 0 else 1
    mw, cm, qt, ol, gm = _gdn_ops(qkv_raw, gates, w["cw8"], w["prm"], nck)
    o_f, o_b = _gdn_scan(mw, cm, qt, ol, gm, _pick(s // CHUNK, 8))
    return _post(x, mod, modf, o_mla, o_f, o_b, z, w["ggdn"], w["wout"], w["gmlp"], w["w1"], w["w2"],
                 w["gfin"], ts)


def kernel(x_prompt, x_sample, c_prompt, c_sample, w_ada, b_ada, g_mix, w_in, g_q, w_uq, g_kv, w_ukv,
           conv_w, a_log_f, a_log_b, dt_f, dt_b, g_gdn, w_out, g_mlp, w_mlp_in, w_mlp_out,
           w_ada_f, b_ada_f, g_final):
    d = D_MODEL
    bp, bs = x_prompt.shape[0], x_sample.shape[0]
    c8 = jnp.zeros((8, d), F32).at[:bp].set(c_prompt).at[bp:bp + bs].set(c_sample)
    mod = _ada(c8, w_ada[0], b_ada[0][None, :])[:, None, :]
    modf = _ada(c8, w_ada_f, b_ada_f[None, :])[:, None, :]

    wi = w_in[0]
    o_kr = Q_LORA + KV_LORA
    o_qkv = o_kr + QK_ROPE
    o_z = o_qkv + GDN_CH
    o_g = o_z + GDN_W
    kr = wi[:, o_kr:o_qkv]
    wa = jnp.concatenate([wi[:, :o_kr], kr, _swap_halves(kr), wi[:, o_qkv:o_g], wi[:, o_g:],
                          jnp.zeros((d, GATE_PAD - 4 * GDN_HEADS), F32)], axis=1).astype(BF16)
    wq = w_uq[0].reshape(Q_LORA, MLA_HEADS, QK_NOPE + QK_ROPE)
    wq_r = wq[..., QK_NOPE:]
    wuq = jnp.concatenate([wq[..., :QK_NOPE], wq_r, _swap_halves(wq_r)], axis=-1)
    wuq = wuq.reshape(Q_LORA, MLA_HEADS * QK_PAD).astype(BF16)
    lanes = jnp.zeros((128,), F32)
    prm = jnp.zeros((8, 128), F32)
    prm = prm.at[0].set(lanes.at[0:4].set(a_log_f[0]).at[4:8].set(a_log_b[0]))
    prm = prm.at[1].set(lanes.at[0:4].set(dt_f[0]).at[4:8].set(dt_b[0]))
    w = dict(
        gmix=g_mix, wa=wa, gq=g_q, wuq=wuq, gkv=g_kv, wukv=w_ukv[0].astype(BF16),
        cw8=jnp.zeros((8, GDN_CH), F32).at[:CONV_W].set(conv_w[0]), prm=prm,
        ggdn=g_gdn, wout=w_out[0].astype(BF16), gmlp=g_mlp,
        w1=w_mlp_in[0].astype(BF16), w2=w_mlp_out[0].astype(BF16), gfin=g_final[None, :],
    )
    y_p = _trunk(x_prompt, mod[:bp], modf[:bp], w)
    y_s = _trunk(x_sample, mod[bp:bp + bs], modf[bp:bp + bs], w)
    return (y_p, y_s)
```

```python
import functools

import jax
import jax.numpy as jnp
from jax import lax
from jax.experimental import pallas as pl
from jax.experimental.pallas import tpu as pltpu

F32 = jnp.float32
BF16 = jnp.bfloat16

EPS = 1e-6
LOG2_E = 1.4426950408889634
D_MODEL = 1024
N_MOD = 6
MLA_HEADS = 4
QK_NOPE = 128
QK_ROPE = 64
V_DIM = 128
Q_LORA = 384
KV_LORA = 256
ROPE_THETA = 10000.0
GDN_HEADS = 4
GDN_DK = 128
GDN_DV = 128
CONV_W = 5
D_FF = 4 * D_MODEL
QK_PAD = 256
VT_ROWS = V_DIM + 16
GDN_CH = 2 * GDN_HEADS * GDN_DK + GDN_HEADS * GDN_DV
GDN_W = GDN_HEADS * GDN_DV
MLA_W = MLA_HEADS * V_DIM
GATE_PAD = 128
C_CQ = 0
C_CKV = C_CQ + Q_LORA
C_KR = C_CKV + KV_LORA
C_QKV = C_KR + 128
C_Z = C_QKV + GDN_CH
C_GATE = C_Z + GDN_W
WA_COLS = C_GATE + GATE_PAD

CHUNK = 128
INV_BLK = 16
VMEM_LIMIT = 56 * 1024 * 1024


def _dot(a, b):
    return jnp.dot(a, b, preferred_element_type=F32)


def _dot_bf(a, b):
    return jnp.dot(a.astype(BF16), b.astype(BF16), preferred_element_type=F32)


def _split2(a):
    hi = a.astype(BF16)
    lo = (a - hi.astype(F32)).astype(BF16)
    return hi, lo


def _dot_x3(a, b):
    ah, al = _split2(a)
    bh, bl = _split2(b)
    return _dot(ah, bh) + (_dot(ah, bl) + _dot(al, bh))


def _sigmoid(x):
    return 1.0 / (1.0 + jnp.exp(-x))


def _silu(x):
    return x * _sigmoid(x)


def _cparams(sem):
    return pltpu.CompilerParams(dimension_semantics=sem, vmem_limit_bytes=VMEM_LIMIT)


def _ada_kernel(c_ref, w_ref, b_ref, o_ref):
    sc = _silu(c_ref[...])
    o_ref[...] = _dot_bf(sc, w_ref[...]) + b_ref[...]


def _ada(c8, w, b):
    d, n = w.shape
    tn = 1024
    return pl.pallas_call(
        _ada_kernel,
        out_shape=jax.ShapeDtypeStruct((8, n), F32),
        grid=(n // tn,),
        in_specs=[pl.BlockSpec((8, d), lambda j: (0, 0)),
                  pl.BlockSpec((d, tn), lambda j: (0, j)),
                  pl.BlockSpec((1, tn), lambda j: (0, j))],
        out_specs=pl.BlockSpec((8, tn), lambda j: (0, j)),
        compiler_params=_cparams(("parallel",)),
        name="ada",
    )(c8, w, b)


def _rms(x, g):
    ms = jnp.mean(x * x, axis=-1, keepdims=True)
    return x * lax.rsqrt(ms + EPS) * g


def _pre_kernel(x_ref, mod_ref, gmix_ref, wa_ref, gq_ref, wuq_ref, gkv_ref, wukv_ref, cs_ref, sn_ref,
                q_ref, k_ref, v_ref, qkv_ref, z_ref, gate_ref):
    d = D_MODEL
    x = x_ref[0]
    mod = mod_ref[0]
    shift = mod[:, 0:d]
    scale = mod[:, d:2 * d]
    hb = (_rms(x, gmix_ref[...]) * (1.0 + scale) + shift).astype(BF16)

    def proj(lo, hi):
        return _dot(hb, wa_ref[:, lo:hi])

    cs = cs_ref[...]
    sn = sn_ref[...]

    def rope(t):
        return t * cs + pltpu.roll(t, 64, 1) * sn

    qk_scale = (QK_NOPE + QK_ROPE) ** -0.5 * LOG2_E
    ones = jnp.ones((VT_ROWS - V_DIM, x.shape[0]), BF16)
    cq = _rms(proj(C_CQ, C_CKV), gq_ref[...])
    q = _dot_bf(cq, wuq_ref[...])
    ckv = _rms(proj(C_CKV, C_KR), gkv_ref[...])
    kv = _dot_bf(ckv, wukv_ref[...])
    kr = rope(proj(C_KR, C_QKV)).astype(BF16)
    for h in range(MLA_HEADS):
        o = h * QK_PAD
        q_ref[0, h, 0:128, :] = (q[:, o:o + 128] * qk_scale).T.astype(BF16)
        q_ref[0, h, 128:256, :] = (rope(q[:, o + 128:o + 256]) * qk_scale).T.astype(BF16)
        k_ref[0, h, :, 0:128] = kv[:, o:o + 128].astype(BF16)
        k_ref[0, h, :, 128:256] = kr
        v_ref[0, h, 0:V_DIM, :] = kv[:, o + 128:o + 256].T.astype(BF16)
        v_ref[0, h, V_DIM:VT_ROWS, :] = ones
    qkv_ref[0] = proj(C_QKV, C_Z)
    z_ref[0] = proj(C_Z, C_GATE)
    gate_ref[0] = proj(C_GATE, WA_COLS)


def _pre(x, mod, gmix, wa, gq, wuq, gkv, wukv, cs, sn, ts):
    b, s, d = x.shape
    h = MLA_HEADS
    const = lambda bi, si: (0, 0)
    return pl.pallas_call(
        _pre_kernel,
        out_shape=(jax.ShapeDtypeStruct((b, h, QK_PAD, s), BF16),
                   jax.ShapeDtypeStruct((b, h, s, QK_PAD), BF16),
                   jax.ShapeDtypeStruct((b, h, VT_ROWS, s), BF16),
                   jax.ShapeDtypeStruct((b, s, GDN_CH), F32),
                   jax.ShapeDtypeStruct((b, s, GDN_W), F32),
                   jax.ShapeDtypeStruct((b, s, GATE_PAD), F32)),
        grid=(b, s // ts),
        in_specs=[pl.BlockSpec((1, ts, d), lambda bi, si: (bi, si, 0)),
                  pl.BlockSpec((1, 1, N_MOD * d), lambda bi, si: (bi, 0, 0)),
                  pl.BlockSpec((1, d), const),
                  pl.BlockSpec((d, WA_COLS), const, pipeline_mode=pl.Buffered(1)),
                  pl.BlockSpec((1, Q_LORA), const),
                  pl.BlockSpec((Q_LORA, h * QK_PAD), const),
                  pl.BlockSpec((1, KV_LORA), const),
                  pl.BlockSpec((KV_LORA, h * 256), const),
                  pl.BlockSpec((ts, 128), lambda bi, si: (si, 0)),
                  pl.BlockSpec((ts, 128), lambda bi, si: (si, 0))],
        out_specs=(pl.BlockSpec((1, h, QK_PAD, ts), lambda bi, si: (bi, 0, 0, si)),
                   pl.BlockSpec((1, h, ts, QK_PAD), lambda bi, si: (bi, 0, si, 0)),
                   pl.BlockSpec((1, h, VT_ROWS, ts), lambda bi, si: (bi, 0, 0, si)),
                   pl.BlockSpec((1, ts, GDN_CH), lambda bi, si: (bi, si, 0)),
                   pl.BlockSpec((1, ts, GDN_W), lambda bi, si: (bi, si, 0)),
                   pl.BlockSpec((1, ts, GATE_PAD), lambda bi, si: (bi, si, 0))),
        compiler_params=_cparams(("parallel", "parallel")),
        name="pre",
    )(x, mod, gmix, wa, gq, wuq, gkv, wukv, cs, sn)


def _attn_kernel(qt_ref, k_ref, vt_ref, o_ref, s_ref, p_ref, acc_ref, m_ref, al_ref, mt_ref, *, tk, nk):
    qt = qt_ref[0, 0]

    def scores(i, slot):
        s = _dot(k_ref[0, 0, pl.ds(pl.multiple_of(i * tk, tk), tk), :], qt)
        s_ref[slot] = s
        mt_ref[slot] = jnp.max(s, axis=0, keepdims=True)

    def flush(i, slot):
        vt = vt_ref[0, 0, :, pl.ds(pl.multiple_of(i * tk, tk), tk)]
        acc_ref[...] = al_ref[slot] * acc_ref[...] + _dot(vt, p_ref[slot])

    def softmax(cur):
        m = m_ref[...]
        m_new = jnp.maximum(m, mt_ref[cur])
        al_ref[cur] = jnp.exp2(m - m_new)
        m_ref[...] = m_new
        p_ref[cur] = jnp.exp2(s_ref[cur] - m_new).astype(BF16)

    def step(i, cur):
        softmax(cur)
        flush(i - 1, 1 - cur)
        scores(i + 1, 1 - cur)

    m_ref[...] = jnp.full(m_ref.shape, -jnp.inf, F32)
    acc_ref[...] = jnp.zeros(acc_ref.shape, F32)
    scores(0, 0)
    softmax(0)
    scores(1, 1)

    def pair(j, carry):
        step(2 * j + 1, 1)

        @pl.when(j >= 0)
        def _():
            step(2 * j + 2, 0)

        return carry

    lax.fori_loop(0, nk // 2 - 1, pair, 0)
    softmax(1)
    flush(nk - 2, 0)
    flush(nk - 1, 1)
    acc = acc_ref[...]
    o_ref[0] = (acc[:V_DIM] / acc[V_DIM:V_DIM + 1]).T.astype(o_ref.dtype)


def _attention(qt, k, vt, tq, tk):
    b, h, s, _ = k.shape
    assert (s // tk) % 2 == 0
    return pl.pallas_call(
        functools.partial(_attn_kernel, tk=tk, nk=s // tk),
        out_shape=jax.ShapeDtypeStruct((b, s, h * V_DIM), BF16),
        grid=(b, h, s // tq),
        in_specs=[pl.BlockSpec((1, 1, QK_PAD, tq), lambda bi, hi, qi: (bi, hi, 0, qi)),
                  pl.BlockSpec((1, 1, s, QK_PAD), lambda bi, hi, qi: (bi, hi, 0, 0)),
                  pl.BlockSpec((1, 1, VT_ROWS, s), lambda bi, hi, qi: (bi, hi, 0, 0))],
        out_specs=pl.BlockSpec((1, tq, V_DIM), lambda bi, hi, qi: (bi, qi, hi)),
        scratch_shapes=[pltpu.VMEM((2, tk, tq), F32), pltpu.VMEM((2, tk, tq), BF16),
                        pltpu.VMEM((VT_ROWS, tq), F32), pltpu.VMEM((1, tq), F32), pltpu.VMEM((2, 1, tq), F32),
                        pltpu.VMEM((2, 1, tq), F32)],
        compiler_params=_cparams(("parallel", "parallel", "parallel")),
        name="attn",
    )(qt, k, vt)


def _bf(x):
    return x.astype(BF16)


def _cat(a, b):
    return jnp.concatenate([a, b], axis=1)


def _tri_solve_many(a_list, rhs_list, row, col):
    c = CHUNK
    eye = (row == col).astype(F32)
    blk = (row // INV_BLK) == (col // INV_BLK)
    n = range(len(a_list))
    x0 = [jnp.where(blk, -a, 0.0) for a in a_list]
    e = [_bf(jnp.where(blk, 0.0, a)) for a in a_list]
    xb = [_bf(x) for x in x0]
    x1 = [_dot(xb[i], xb[i]) for i in n]
    s = [eye + x0[i] for i in n]
    xb = [_bf(x) for x in x1]
    r = [_dot(xb[i], _cat(xb[i], _bf(s[i]))) for i in n]
    s = [s[i] + r[i][:, c:] for i in n]
    xb = [_bf(r[i][:, :c]) for i in n]
    r = [_dot(xb[i], _cat(xb[i], _bf(s[i]))) for i in n]
    s = [s[i] + r[i][:, c:] for i in n]
    xb = [_bf(r[i][:, :c]) for i in n]
    sb = [_bf(s[i]) for i in n]
    s = [s[i] + _dot(xb[i], sb[i]) for i in n]
    sb = [_bf(s[i]) for i in n]
    y0 = [-_dot(sb[i], e[i]) for i in n]
    dr = [_dot(sb[i], _bf(rhs_list[i])) for i in n]
    yb = [_bf(y) for y in y0]
    y1 = [_dot(yb[i], yb[i]) for i in n]
    t = [eye + y0[i] for i in n]
    yb = [_bf(y) for y in y1]
    r = [_dot(yb[i], _cat(yb[i], _bf(t[i]))) for i in n]
    t = [t[i] + r[i][:, c:] for i in n]
    yb = [_bf(r[i][:, :c]) for i in n]
    tb = [_bf(t[i]) for i in n]
    t = [t[i] + _dot(yb[i], tb[i]) for i in n]
    return [_dot(_bf(t[i]), _bf(dr[i])) for i in n]


def _gdn_kernel(prev_ref, main_ref, next_ref, gate_ref, cw_ref, prm_ref,
                mw_ref, cm_ref, qt_ref, ol_ref, gm_ref, xe_ref, *, nck):
    tc = nck * CHUNK
    si = pl.program_id(1)
    ns = pl.num_programs(1)
    hh = GDN_HEADS
    xe_ref[0:8, :] = jnp.where(si > 0, prev_ref[0], 0.0)
    xe_ref[8:8 + tc, :] = main_ref[0]
    xe_ref[8 + tc:16 + tc, :] = jnp.where(si < ns - 1, next_ref[0], 0.0)
    pad = CONV_W // 2
    xe = xe_ref[...]
    acc = cw_ref[pad:pad + 1, :] * xe[8:8 + tc]
    for j in range(CONV_W):
        if j != pad:
            acc = acc + cw_ref[j:j + 1, :] * pltpu.roll(xe, (pad - j) % (tc + 16), 0)[8:8 + tc]
    qkv = _silu(acc)
    gs = gate_ref[0]
    a = gs + prm_ref[1:2, :]
    softplus = jnp.maximum(a, 0.0) + jnp.log1p(jnp.exp(-jnp.abs(a)))
    lane = lax.broadcasted_iota(jnp.int32, (tc, 128), 1)
    glog = jnp.where(lane < 2 * hh, -jnp.exp(prm_ref[0:1, :]) * softplus, 0.0)
    beta = _sigmoid(gs)
    g1 = _bf(glog).astype(F32)
    r1 = glog - g1
    g2 = _bf(r1).astype(F32)
    g3 = r1 - g2
    gpk = _bf(g1 + pltpu.roll(g2, 8, 1) + pltpu.roll(g3, 16, 1))

    row = lax.broadcasted_iota(jnp.int32, (CHUNK, CHUNK), 0)
    col = lax.broadcasted_iota(jnp.int32, (CHUNK, CHUNK), 1)
    low_bf = (row >= col).astype(BF16)
    up_bf = (row <= col).astype(BF16)
    masks = ((row >= col, row > col), (row <= col, row < col))

    a_list, rhs_list, inst = [], [], []
    for c in range(nck):
        r0 = c * CHUNK
        pk = gpk[r0:r0 + CHUNK, :]
        pre = _dot(low_bf, pk)
        suf = _dot(up_bf, pk)
        gsum = jnp.where(col < hh, pre, suf)
        gsum = gsum + pltpu.roll(gsum, 120, 1) + pltpu.roll(gsum, 112, 1)
        gsum_t = gsum.T
        for h in range(hh):
            qh = qkv[r0:r0 + CHUNK, h * GDN_DK:(h + 1) * GDN_DK]
            kh = qkv[r0:r0 + CHUNK, (hh + h) * GDN_DK:(hh + h + 1) * GDN_DK]
            vh = qkv[r0:r0 + CHUNK, 2 * hh * GDN_DK + h * GDN_DV:2 * hh * GDN_DK + (h + 1) * GDN_DV]
            qh = qh * (lax.rsqrt(jnp.sum(qh * qh, axis=-1, keepdims=True) + EPS) * (GDN_DK ** -0.5))
            kh = kh * lax.rsqrt(jnp.sum(kh * kh, axis=-1, keepdims=True) + EPS)
            kb = _bf(kh)
            gram = lax.dot_general(jnp.concatenate([kb, _bf(qh)], axis=0), kb,
                                   (((1,), (1,)), ((), ())), preferred_element_type=F32)
            kk = gram[:CHUNK]
            qk = gram[CHUNK:]
            for d in range(2):
                idx = d * hh + h
                incl, strict = masks[d]
                g_c = gsum[:, idx:idx + 1]
                g_r = gsum_t[idx:idx + 1, :]
                decay = jnp.where(incl, jnp.exp(jnp.where(incl, g_c - g_r, 0.0)), 0.0)
                b_c = beta[r0:r0 + CHUNK, 2 * hh + idx:2 * hh + idx + 1]
                eg = jnp.exp(g_c)
                g_tot = g_c[CHUNK - 1:CHUNK, :] if d == 0 else g_c[0:1, :]
                a_list.append(jnp.where(strict, (b_c * kk) * decay, 0.0))
                rhs_list.append(_cat(vh * b_c, kh * (b_c * eg)))
                inst.append((c, h, d, idx, _bf(jnp.where(incl, qk * decay, 0.0)),
                             _bf((kh * jnp.exp(g_tot - g_c)).T), qh * eg, jnp.exp(g_tot)))
    uw = _tri_solve_many(a_list, rhs_list, row, col)
    uwb = [_bf(x) for x in uw]
    au = [_dot(inst[i][4], uwb[i]) for i in range(len(inst))]
    kt = [_dot(inst[i][5], uwb[i]) for i in range(len(inst))]
    for i, (c, h, d, idx, _, _, qdec, gam) in enumerate(inst):
        ol_ref[0, c, d, h] = au[i][:, 0:GDN_DV]
        qt_ref[0, c, d, h] = _bf(qdec - au[i][:, GDN_DV:])
        cm_ref[0, c, d, h] = kt[i][:, 0:GDN_DV]
        mw_ref[0, c, d, h] = _bf(kt[i][:, GDN_DV:])
        gm_ref[0, c, idx:idx + 1, :] = jnp.broadcast_to(gam, (1, 128))


def _gdn_ops(qkv_raw, gates, cw8, prm, nck):
    b, s, _ = qkv_raw.shape
    tc = nck * CHUNK
    n = s // CHUNK
    r8 = tc // 8
    nb8 = s // 8
    op_shape = jax.ShapeDtypeStruct((b, n, 2, GDN_HEADS, CHUNK, 128), F32)
    op_bf16 = jax.ShapeDtypeStruct(op_shape.shape, BF16)
    op_spec = pl.BlockSpec((1, nck, 2, GDN_HEADS, CHUNK, 128), lambda bi, si: (bi, si, 0, 0, 0, 0))
    return pl.pallas_call(
        functools.partial(_gdn_kernel, nck=nck),
        out_shape=(op_bf16, op_shape, op_bf16, op_shape,
                   jax.ShapeDtypeStruct((b, n, 8, 128), F32)),
        grid=(b, s // tc),
        in_specs=[pl.BlockSpec((1, 8, GDN_CH), lambda bi, si: (bi, jnp.maximum(si * r8 - 1, 0), 0)),
                  pl.BlockSpec((1, tc, GDN_CH), lambda bi, si: (bi, si, 0)),
                  pl.BlockSpec((1, 8, GDN_CH), lambda bi, si: (bi, jnp.minimum((si + 1) * r8, nb8 - 1), 0)),
                  pl.BlockSpec((1, tc, GATE_PAD), lambda bi, si: (bi, si, 0)),
                  pl.BlockSpec((8, GDN_CH), lambda bi, si: (0, 0)),
                  pl.BlockSpec((8, 128), lambda bi, si: (0, 0))],
        out_specs=(op_spec, op_spec, op_spec, op_spec,
                   pl.BlockSpec((1, nck, 8, 128), lambda bi, si: (bi, si, 0, 0))),
        scratch_shapes=[pltpu.VMEM((tc + 16, GDN_CH), F32)],
        compiler_params=_cparams(("parallel", "parallel")),
        name="gdn_ops",
    )(qkv_raw, qkv_raw, qkv_raw, gates, cw8, prm)


def _scan_kernel(mwf, cmf, qtf, olf, gmf, mwb, cmb, qtb, olb, gmb, of_ref, ob_ref, s_ref, *, ks):
    @pl.when(pl.program_id(1) == 0)
    def _():
        s_ref[...] = jnp.zeros_like(s_ref)

    hh = GDN_HEADS
    for k in range(ks):
        for d, (mw, cm, qt, ol, gm, o_ref, kk) in enumerate(((mwf, cmf, qtf, olf, gmf, of_ref, k),
                                                             (mwb, cmb, qtb, olb, gmb, ob_ref, ks - 1 - k))):
            for h in range(hh):
                idx = d * hh + h
                st = s_ref[idx]
                sb = st.astype(BF16)
                o_ref[0, kk * CHUNK:(kk + 1) * CHUNK, h * GDN_DV:(h + 1) * GDN_DV] = (
                    _dot(qt[0, kk, 0, h], sb) + ol[0, kk, 0, h])
                gam = gm[0, kk, idx:idx + 1, :]
                s_ref[idx] = st * gam - _dot(mw[0, kk, 0, h], sb) + cm[0, kk, 0, h]


def _gdn_scan(mw, cm, qt, ol, gm, ks):
    b, n = mw.shape[0], mw.shape[1]
    s = n * CHUNK
    nb = n // ks
    blk = (1, ks, 1, GDN_HEADS, CHUNK, 128)
    fwd = lambda bi, ni: (bi, ni, 0, 0, 0, 0)
    bwd = lambda bi, ni: (bi, nb - 1 - ni, 1, 0, 0, 0)
    gfwd = lambda bi, ni: (bi, ni, 0, 0)
    gbwd = lambda bi, ni: (bi, nb - 1 - ni, 0, 0)
    o_shape = jax.ShapeDtypeStruct((b, s, GDN_W), F32)
    return pl.pallas_call(
        functools.partial(_scan_kernel, ks=ks),
        out_shape=(o_shape, o_shape),
        grid=(b, nb),
        in_specs=[pl.BlockSpec(blk, fwd)] * 4 + [pl.BlockSpec((1, ks, 8, 128), gfwd)]
        + [pl.BlockSpec(blk, bwd)] * 4 + [pl.BlockSpec((1, ks, 8, 128), gbwd)],
        out_specs=(pl.BlockSpec((1, ks * CHUNK, GDN_W), lambda bi, ni: (bi, ni, 0)),
                   pl.BlockSpec((1, ks * CHUNK, GDN_W), lambda bi, ni: (bi, nb - 1 - ni, 0))),
        scratch_shapes=[pltpu.VMEM((2 * GDN_HEADS, GDN_DK, GDN_DV), F32)],
        compiler_params=_cparams(("parallel", "arbitrary")),
        name="gdn_scan",
    )(mw, cm, qt, ol, gm, mw, cm, qt, ol, gm)


def _post_kernel(x_ref, mod_ref, modf_ref, om_ref, of_ref, ob_ref, z_ref, ggdn_ref, wout_ref,
                 gmlp_ref, w1_ref, w2_ref, gfin_ref, y_ref, *, ff_blk):
    d = D_MODEL
    x = x_ref[0]
    mod = mod_ref[0]
    gate_a = mod[:, 2 * d:3 * d]
    shift_m = mod[:, 3 * d:4 * d]
    scale_m = mod[:, 4 * d:5 * d]
    gate_m = mod[:, 5 * d:6 * d]
    modf = modf_ref[0]
    shift_f = modf[:, 0:d]
    scale_f = modf[:, d:2 * d]

    o = of_ref[0] + ob_ref[0]
    z = z_ref[0]
    mixed = _dot(om_ref[0], wout_ref[0:MLA_W, :])
    for h in range(GDN_HEADS):
        sl = slice(h * GDN_DV, (h + 1) * GDN_DV)
        og = _rms(o[:, sl], ggdn_ref[...]) * _silu(z[:, sl])
        mixed = mixed + _dot(og.astype(BF16), wout_ref[MLA_W + h * GDN_DV:MLA_W + (h + 1) * GDN_DV, :])
    x1 = x + gate_a * mixed
    hb = (_rms(x1, gmlp_ref[...]) * (1.0 + scale_m) + shift_m).astype(BF16)
    acc = jnp.zeros_like(x1)
    for j in range(D_FF // ff_blk):
        hid = _dot(hb, w1_ref[:, j * ff_blk:(j + 1) * ff_blk])
        hid = jnp.square(jnp.maximum(hid, 0.0)).astype(BF16)
        acc = acc + _dot(hid, w2_ref[j * ff_blk:(j + 1) * ff_blk, :])
    x2 = x1 + gate_m * acc
    y_ref[0] = _rms(x2, gfin_ref[...]) * (1.0 + scale_f) + shift_f


def _post(x, mod, modf, om, of, ob, z, ggdn, wout, gmlp, w1, w2, gfin, ts):
    b, s, d = x.shape
    const = lambda bi, si: (0, 0)
    tok = lambda w: pl.BlockSpec((1, ts, w), lambda bi, si: (bi, si, 0))
    one = pl.Buffered(1)
    return pl.pallas_call(
        functools.partial(_post_kernel, ff_blk=1024),
        out_shape=jax.ShapeDtypeStruct((b, s, d), F32),
        grid=(b, s // ts),
        in_specs=[tok(d),
                  pl.BlockSpec((1, 1, N_MOD * d), lambda bi, si: (bi, 0, 0)),
                  pl.BlockSpec((1, 1, 2 * d), lambda bi, si: (bi, 0, 0)),
                  tok(MLA_W), tok(GDN_W), tok(GDN_W), tok(GDN_W),
                  pl.BlockSpec((1, GDN_DV), const),
                  pl.BlockSpec((MLA_W + GDN_W, d), const, pipeline_mode=one),
                  pl.BlockSpec((1, d), const),
                  pl.BlockSpec((d, D_FF), const, pipeline_mode=one),
                  pl.BlockSpec((D_FF, d), const, pipeline_mode=one),
                  pl.BlockSpec((1, d), const)],
        out_specs=tok(d),
        compiler_params=_cparams(("parallel", "parallel")),
        name="post",
    )(x, mod, modf, om, of, ob, z, ggdn, wout, gmlp, w1, w2, gfin)


def _swap_halves(t):
    half = t.shape[-1] // 2
    return jnp.concatenate([t[..., half:], t[..., :half]], axis=-1)


def _rope_tables(s):
    inv = 1.0 / (ROPE_THETA ** (jnp.arange(0, QK_ROPE, 2, dtype=F32) / QK_ROPE))
    ang = jnp.arange(s, dtype=F32)[:, None] * inv[None, :]
    cos, sin = jnp.cos(ang), jnp.sin(ang)
    zeros = jnp.zeros((s, 128 - QK_ROPE), F32)
    return (jnp.concatenate([cos, cos, zeros], axis=1),
            jnp.concatenate([-sin, sin, zeros], axis=1))


def _pick(n, pref):
    t = min(n, pref)
    while n % t:
        t //= 2
    return t


def _trunk(x, mod, modf, w):
    b, s, d = x.shape
    cs, sn = _rope_tables(s)
    ts = _pick(s, 512)
    q, k, v, qkv_raw, z, gates = _pre(x, mod, w["gmix"], w["wa"], w["gq"], w["wuq"], w["gkv"], w["wukv"],
                                      cs, sn, ts)
    o_mla = _attention(q, k, v, _pick(s, 1024), _pick(s // 2, 1024))
    nck = 2 if s % (2 * CHUNK) == 0 else 1
    mw, cm, qt, ol, gm = _gdn_ops(qkv_raw, gates, w["cw8"], w["prm"], nck)
    o_f, o_b = _gdn_scan(mw, cm, qt, ol, gm, _pick(s // CHUNK, 8))
    return _post(x, mod, modf, o_mla, o_f, o_b, z, w["ggdn"], w["wout"], w["gmlp"], w["w1"], w["w2"],
                 w["gfin"], ts)


def kernel(x_prompt, x_sample, c_prompt, c_sample, w_ada, b_ada, g_mix, w_in, g_q, w_uq, g_kv, w_ukv,
           conv_w, a_log_f, a_log_b, dt_f, dt_b, g_gdn, w_out, g_mlp, w_mlp_in, w_mlp_out,
           w_ada_f, b_ada_f, g_final):
    d = D_MODEL
    bp, bs = x_prompt.shape[0], x_sample.shape[0]
    c8 = jnp.zeros((8, d), F32).at[:bp].set(c_prompt).at[bp:bp + bs].set(c_sample)
    mod = _ada(c8, w_ada[0], b_ada[0][None, :])[:, None, :]
    modf = _ada(c8, w_ada_f, b_ada_f[None, :])[:, None, :]

    wi = w_in[0]
    o_kr = Q_LORA + KV_LORA
    o_qkv = o_kr + QK_ROPE
    o_z = o_qkv + GDN_CH
    o_g = o_z + GDN_W
    kr = wi[:, o_kr:o_qkv]
    wa = jnp.concatenate([wi[:, :o_kr], kr, _swap_halves(kr), wi[:, o_qkv:o_g], wi[:, o_g:],
                          jnp.zeros((d, GATE_PAD - 4 * GDN_HEADS), F32)], axis=1).astype(BF16)
    wq = w_uq[0].reshape(Q_LORA, MLA_HEADS, QK_NOPE + QK_ROPE)
    wq_r = wq[..., QK_NOPE:]
    wuq = jnp.concatenate([wq[..., :QK_NOPE], wq_r, _swap_halves(wq_r)], axis=-1)
    wuq = wuq.reshape(Q_LORA, MLA_HEADS * QK_PAD).astype(BF16)
    lanes = jnp.zeros((128,), F32)
    prm = jnp.zeros((8, 128), F32)
    prm = prm.at[0].set(lanes.at[0:4].set(a_log_f[0]).at[4:8].set(a_log_b[0]))
    prm = prm.at[1].set(lanes.at[0:4].set(dt_f[0]).at[4:8].set(dt_b[0]))
    w = dict(
        gmix=g_mix, wa=wa, gq=g_q, wuq=wuq, gkv=g_kv, wukv=w_ukv[0].astype(BF16),
        cw8=jnp.zeros((8, GDN_CH), F32).at[:CONV_W].set(conv_w[0]), prm=prm,
        ggdn=g_gdn, wout=w_out[0].astype(BF16), gmlp=g_mlp,
        w1=w_mlp_in[0].astype(BF16), w2=w_mlp_out[0].astype(BF16), gfin=g_final[None, :],
    )
    y_p = _trunk(x_prompt, mod[:bp], modf[:bp], w)
    y_s = _trunk(x_sample, mod[bp:bp + bs], modf[bp:bp + bs], w)
    return (y_p, y_s)
```

```python
import functools

import jax
import jax.numpy as jnp
from jax import lax
from jax.experimental import pallas as pl
from jax.experimental.pallas import tpu as pltpu

F32 = jnp.float32
BF16 = jnp.bfloat16

EPS = 1e-6
LOG2_E = 1.4426950408889634
D_MODEL = 1024
N_MOD = 6
MLA_HEADS = 4
QK_NOPE = 128
QK_ROPE = 64
V_DIM = 128
Q_LORA = 384
KV_LORA = 256
ROPE_THETA = 10000.0
GDN_HEADS = 4
GDN_DK = 128
GDN_DV = 128
CONV_W = 5
D_FF = 4 * D_MODEL
QK_PAD = 256
VT_ROWS = V_DIM + 16
GDN_CH = 2 * GDN_HEADS * GDN_DK + GDN_HEADS * GDN_DV
GDN_W = GDN_HEADS * GDN_DV
MLA_W = MLA_HEADS * V_DIM
GATE_PAD = 128
C_CQ = 0
C_CKV = C_CQ + Q_LORA
C_KR = C_CKV + KV_LORA
C_QKV = C_KR + 128
C_Z = C_QKV + GDN_CH
C_GATE = C_Z + GDN_W
WA_COLS = C_GATE + GATE_PAD

CHUNK = 128
INV_BLK = 16
VMEM_LIMIT = 56 * 1024 * 1024


def _dot(a, b):
    return jnp.dot(a, b, preferred_element_type=F32)


def _dot_bf(a, b):
    return jnp.dot(a.astype(BF16), b.astype(BF16), preferred_element_type=F32)


def _split2(a):
    hi = a.astype(BF16)
    lo = (a - hi.astype(F32)).astype(BF16)
    return hi, lo


def _dot_x3(a, b):
    ah, al = _split2(a)
    bh, bl = _split2(b)
    return _dot(ah, bh) + (_dot(ah, bl) + _dot(al, bh))


def _sigmoid(x):
    return 1.0 / (1.0 + jnp.exp(-x))


def _silu(x):
    return x * _sigmoid(x)


def _cparams(sem):
    return pltpu.CompilerParams(dimension_semantics=sem, vmem_limit_bytes=VMEM_LIMIT)


def _ada_kernel(c_ref, w_ref, b_ref, o_ref):
    sc = _silu(c_ref[...])
    o_ref[...] = _dot_bf(sc, w_ref[...]) + b_ref[...]


def _ada(c8, w, b):
    d, n = w.shape
    tn = 1024
    return pl.pallas_call(
        _ada_kernel,
        out_shape=jax.ShapeDtypeStruct((8, n), F32),
        grid=(n // tn,),
        in_specs=[pl.BlockSpec((8, d), lambda j: (0, 0)),
                  pl.BlockSpec((d, tn), lambda j: (0, j)),
                  pl.BlockSpec((1, tn), lambda j: (0, j))],
        out_specs=pl.BlockSpec((8, tn), lambda j: (0, j)),
        compiler_params=_cparams(("parallel",)),
        name="ada",
    )(c8, w, b)


def _rms(x, g):
    ms = jnp.mean(x * x, axis=-1, keepdims=True)
    return x * lax.rsqrt(ms + EPS) * g


def _pre_kernel(x_ref, mod_ref, gmix_ref, wa_ref, gq_ref, wuq_ref, gkv_ref, wukv_ref, cs_ref, sn_ref,
                q_ref, k_ref, v_ref, qkv_ref, z_ref, gate_ref):
    d = D_MODEL
    x = x_ref[0]
    mod = mod_ref[0]
    shift = mod[:, 0:d]
    scale = mod[:, d:2 * d]
    hb = (_rms(x, gmix_ref[...]) * (1.0 + scale) + shift).astype(BF16)

    def proj(lo, hi):
        return _dot(hb, wa_ref[:, lo:hi])

    cs = cs_ref[...]
    sn = sn_ref[...]

    def rope(t):
        return t * cs + pltpu.roll(t, 64, 1) * sn

    qk_scale = (QK_NOPE + QK_ROPE) ** -0.5 * LOG2_E
    ones = jnp.ones((VT_ROWS - V_DIM, x.shape[0]), BF16)
    cq = _rms(proj(C_CQ, C_CKV), gq_ref[...])
    q = _dot_bf(cq, wuq_ref[...])
    ckv = _rms(proj(C_CKV, C_KR), gkv_ref[...])
    kv = _dot_bf(ckv, wukv_ref[...])
    kr = rope(proj(C_KR, C_QKV)).astype(BF16)
    for h in range(MLA_HEADS):
        o = h * QK_PAD
        q_ref[0, h, 0:128, :] = (q[:, o:o + 128] * qk_scale).T.astype(BF16)
        q_ref[0, h, 128:256, :] = (rope(q[:, o + 128:o + 256]) * qk_scale).T.astype(BF16)
        k_ref[0, h, :, 0:128] = kv[:, o:o + 128].astype(BF16)
        k_ref[0, h, :, 128:256] = kr
        v_ref[0, h, 0:V_DIM, :] = kv[:, o + 128:o + 256].T.astype(BF16)
        v_ref[0, h, V_DIM:VT_ROWS, :] = ones
    qkv_ref[0] = proj(C_QKV, C_Z)
    z_ref[0] = proj(C_Z, C_GATE)
    gate_ref[0] = proj(C_GATE, WA_COLS)


def _pre(x, mod, gmix, wa, gq, wuq, gkv, wukv, cs, sn, ts):
    b, s, d = x.shape
    h = MLA_HEADS
    const = lambda bi, si: (0, 0)
    return pl.pallas_call(
        _pre_kernel,
        out_shape=(jax.ShapeDtypeStruct((b, h, QK_PAD, s), BF16),
                   jax.ShapeDtypeStruct((b, h, s, QK_PAD), BF16),
                   jax.ShapeDtypeStruct((b, h, VT_ROWS, s), BF16),
                   jax.ShapeDtypeStruct((b, s, GDN_CH), F32),
                   jax.ShapeDtypeStruct((b, s, GDN_W), F32),
                   jax.ShapeDtypeStruct((b, s, GATE_PAD), F32)),
        grid=(b, s // ts),
        in_specs=[pl.BlockSpec((1, ts, d), lambda bi, si: (bi, si, 0)),
                  pl.BlockSpec((1, 1, N_MOD * d), lambda bi, si: (bi, 0, 0)),
                  pl.BlockSpec((1, d), const),
                  pl.BlockSpec((d, WA_COLS), const, pipeline_mode=pl.Buffered(1)),
                  pl.BlockSpec((1, Q_LORA), const),
                  pl.BlockSpec((Q_LORA, h * QK_PAD), const),
                  pl.BlockSpec((1, KV_LORA), const),
                  pl.BlockSpec((KV_LORA, h * 256), const),
                  pl.BlockSpec((ts, 128), lambda bi, si: (si, 0)),
                  pl.BlockSpec((ts, 128), lambda bi, si: (si, 0))],
        out_specs=(pl.BlockSpec((1, h, QK_PAD, ts), lambda bi, si: (bi, 0, 0, si)),
                   pl.BlockSpec((1, h, ts, QK_PAD), lambda bi, si: (bi, 0, si, 0)),
                   pl.BlockSpec((1, h, VT_ROWS, ts), lambda bi, si: (bi, 0, 0, si)),
                   pl.BlockSpec((1, ts, GDN_CH), lambda bi, si: (bi, si, 0)),
                   pl.BlockSpec((1, ts, GDN_W), lambda bi, si: (bi, si, 0)),
                   pl.BlockSpec((1, ts, GATE_PAD), lambda bi, si: (bi, si, 0))),
        compiler_params=_cparams(("parallel", "parallel")),
        name="pre",
    )(x, mod, gmix, wa, gq, wuq, gkv, wukv, cs, sn)


def _attn_kernel(qt_ref, k_ref, vt_ref, o_ref, s_ref, p_ref, acc_ref, m_ref, al_ref, mt_ref, *, tk, nk):
    qt = qt_ref[0, 0]

    def scores(i, slot):
        s = _dot(k_ref[0, 0, pl.ds(pl.multiple_of(i * tk, tk), tk), :], qt)
        s_ref[slot] = s
        mt_ref[slot] = jnp.max(s, axis=0, keepdims=True)

    def flush(i, slot):
        vt = vt_ref[0, 0, :, pl.ds(pl.multiple_of(i * tk, tk), tk)]
        acc_ref[...] = al_ref[slot] * acc_ref[...] + _dot(vt, p_ref[slot])

    def softmax(cur):
        m = m_ref[...]
        m_new = jnp.maximum(m, mt_ref[cur])
        al_ref[cur] = jnp.exp2(m - m_new)
        m_ref[...] = m_new
        p_ref[cur] = jnp.exp2(s_ref[cur] - m_new).astype(BF16)

    def step(i, cur):
        softmax(cur)
        flush(i - 1, 1 - cur)
        scores(i + 1, 1 - cur)

    m_ref[...] = jnp.full(m_ref.shape, -jnp.inf, F32)
    acc_ref[...] = jnp.zeros(acc_ref.shape, F32)
    scores(0, 0)
    softmax(0)
    scores(1, 1)

    def pair(j, carry):
        step(2 * j + 1, 1)

        @pl.when(j >= 0)
        def _():
            step(2 * j + 2, 0)

        return carry

    lax.fori_loop(0, nk // 2 - 1, pair, 0)
    softmax(1)
    flush(nk - 2, 0)
    flush(nk - 1, 1)
    acc = acc_ref[...]
    o_ref[0] = (acc[:V_DIM] / acc[V_DIM:V_DIM + 1]).T.astype(o_ref.dtype)


def _attention(qt, k, vt, tq, tk):
    b, h, s, _ = k.shape
    assert (s // tk) % 2 == 0
    return pl.pallas_call(
        functools.partial(_attn_kernel, tk=tk, nk=s // tk),
        out_shape=jax.ShapeDtypeStruct((b, s, h * V_DIM), BF16),
        grid=(b, h, s // tq),
        in_specs=[pl.BlockSpec((1, 1, QK_PAD, tq), lambda bi, hi, qi: (bi, hi, 0, qi)),
                  pl.BlockSpec((1, 1, s, QK_PAD), lambda bi, hi, qi: (bi, hi, 0, 0)),
                  pl.BlockSpec((1, 1, VT_ROWS, s), lambda bi, hi, qi: (bi, hi, 0, 0))],
        out_specs=pl.BlockSpec((1, tq, V_DIM), lambda bi, hi, qi: (bi, qi, hi)),
        scratch_shapes=[pltpu.VMEM((2, tk, tq), F32), pltpu.VMEM((2, tk, tq), BF16),
                        pltpu.VMEM((VT_ROWS, tq), F32), pltpu.VMEM((1, tq), F32), pltpu.VMEM((2, 1, tq), F32),
                        pltpu.VMEM((2, 1, tq), F32)],
        compiler_params=_cparams(("parallel", "parallel", "parallel")),
        name="attn",
    )(qt, k, vt)


def _bf(x):
    return x.astype(BF16)


def _cat(a, b):
    return jnp.concatenate([a, b], axis=1)


def _tri_solve_many(a_list, rhs_list, row, col):
    c = CHUNK
    eye = (row == col).astype(F32)
    blk = (row // INV_BLK) == (col // INV_BLK)
    n = range(len(a_list))
    x0 = [jnp.where(blk, -a, 0.0) for a in a_list]
    e = [_bf(jnp.where(blk, 0.0, a)) for a in a_list]
    xb = [_bf(x) for x in x0]
    x1 = [_dot(xb[i], xb[i]) for i in n]
    s = [eye + x0[i] for i in n]
    xb = [_bf(x) for x in x1]
    r = [_dot(xb[i], _cat(xb[i], _bf(s[i]))) for i in n]
    s = [s[i] + r[i][:, c:] for i in n]
    xb = [_bf(r[i][:, :c]) for i in n]
    r = [_dot(xb[i], _cat(xb[i], _bf(s[i]))) for i in n]
    s = [s[i] + r[i][:, c:] for i in n]
    xb = [_bf(r[i][:, :c]) for i in n]
    sb = [_bf(s[i]) for i in n]
    s = [s[i] + _dot(xb[i], sb[i]) for i in n]
    sb = [_bf(s[i]) for i in n]
    y0 = [-_dot(sb[i], e[i]) for i in n]
    dr = [_dot(sb[i], _bf(rhs_list[i])) for i in n]
    yb = [_bf(y) for y in y0]
    y1 = [_dot(yb[i], yb[i]) for i in n]
    t = [eye + y0[i] for i in n]
    yb = [_bf(y) for y in y1]
    r = [_dot(yb[i], _cat(yb[i], _bf(t[i]))) for i in n]
    t = [t[i] + r[i][:, c:] for i in n]
    yb = [_bf(r[i][:, :c]) for i in n]
    tb = [_bf(t[i]) for i in n]
    t = [t[i] + _dot(yb[i], tb[i]) for i in n]
    return [_dot(_bf(t[i]), _bf(dr[i])) for i in n]


def _gdn_kernel(prev_ref, main_ref, next_ref, gate_ref, cw_ref, prm_ref,
                mw_ref, cm_ref, qt_ref, ol_ref, gm_ref, xe_ref, *, nck):
    tc = nck * CHUNK
    si = pl.program_id(1)
    ns = pl.num_programs(1)
    hh = GDN_HEADS
    xe_ref[0:8, :] = jnp.where(si > 0, prev_ref[0], 0.0)
    xe_ref[8:8 + tc, :] = main_ref[0]
    xe_ref[8 + tc:16 + tc, :] = jnp.where(si < ns - 1, next_ref[0], 0.0)
    pad = CONV_W // 2
    xe = xe_ref[...]
    acc = cw_ref[pad:pad + 1, :] * xe[8:8 + tc]
    for j in range(CONV_W):
        if j != pad:
            acc = acc + cw_ref[j:j + 1, :] * pltpu.roll(xe, (pad - j) % (tc + 16), 0)[8:8 + tc]
    qkv = _silu(acc)
    gs = gate_ref[0]
    a = gs + prm_ref[1:2, :]
    softplus = jnp.maximum(a, 0.0) + jnp.log1p(jnp.exp(-jnp.abs(a)))
    lane = lax.broadcasted_iota(jnp.int32, (tc, 128), 1)
    glog = jnp.where(lane < 2 * hh, -jnp.exp(prm_ref[0:1, :]) * softplus, 0.0)
    beta = _sigmoid(gs)
    g1 = _bf(glog).astype(F32)
    r1 = glog - g1
    g2 = _bf(r1).astype(F32)
    g3 = r1 - g2
    gpk = _bf(g1 + pltpu.roll(g2, 8, 1) + pltpu.roll(g3, 16, 1))

    row = lax.broadcasted_iota(jnp.int32, (CHUNK, CHUNK), 0)
    col = lax.broadcasted_iota(jnp.int32, (CHUNK, CHUNK), 1)
    low_bf = (row >= col).astype(BF16)
    up_bf = (row <= col).astype(BF16)
    masks = ((row >= col, row > col), (row <= col, row < col))

    a_list, rhs_list, inst = [], [], []
    for c in range(nck):
        r0 = c * CHUNK
        pk = gpk[r0:r0 + CHUNK, :]
        pre = _dot(low_bf, pk)
        suf = _dot(up_bf, pk)
        gsum = jnp.where(col < hh, pre, suf)
        gsum = gsum + pltpu.roll(gsum, 120, 1) + pltpu.roll(gsum, 112, 1)
        gsum_t = gsum.T
        for h in range(hh):
            qh = qkv[r0:r0 + CHUNK, h * GDN_DK:(h + 1) * GDN_DK]
            kh = qkv[r0:r0 + CHUNK, (hh + h) * GDN_DK:(hh + h + 1) * GDN_DK]
            vh = qkv[r0:r0 + CHUNK, 2 * hh * GDN_DK + h * GDN_DV:2 * hh * GDN_DK + (h + 1) * GDN_DV]
            qh = qh * (lax.rsqrt(jnp.sum(qh * qh, axis=-1, keepdims=True) + EPS) * (GDN_DK ** -0.5))
            kh = kh * lax.rsqrt(jnp.sum(kh * kh, axis=-1, keepdims=True) + EPS)
            kb = _bf(kh)
            gram = lax.dot_general(jnp.concatenate([kb, _bf(qh)], axis=0), kb,
                                   (((1,), (1,)), ((), ())), preferred_element_type=F32)
            kk = gram[:CHUNK]
            qk = gram[CHUNK:]
            for d in range(2):
                idx = d * hh + h
                incl, strict = masks[d]
                g_c = gsum[:, idx:idx + 1]
                g_r = gsum_t[idx:idx + 1, :]
                decay = jnp.where(incl, jnp.exp(jnp.where(incl, g_c - g_r, 0.0)), 0.0)
                b_c = beta[r0:r0 + CHUNK, 2 * hh + idx:2 * hh + idx + 1]
                eg = jnp.exp(g_c)
                g_tot = g_c[CHUNK - 1:CHUNK, :] if d == 0 else g_c[0:1, :]
                a_list.append(jnp.where(strict, (b_c * kk) * decay, 0.0))
                rhs_list.append(_cat(vh * b_c, kh * (b_c * eg)))
                inst.append((c, h, d, idx, _bf(jnp.where(incl, qk * decay, 0.0)),
                             _bf((kh * jnp.exp(g_tot - g_c)).T), qh * eg, jnp.exp(g_tot)))
    uw = _tri_solve_many(a_list, rhs_list, row, col)
    uwb = [_bf(x) for x in uw]
    au = [_dot(inst[i][4], uwb[i]) for i in range(len(inst))]
    kt = [_dot(inst[i][5], uwb[i]) for i in range(len(inst))]
    for i, (c, h, d, idx, _, _, qdec, gam) in enumerate(inst):
        ol_ref[0, c, d, h] = _bf(au[i][:, 0:GDN_DV])
        qt_ref[0, c, d, h] = _bf(qdec - au[i][:, GDN_DV:])
        cm_ref[0, c, d, h] = _bf(kt[i][:, 0:GDN_DV])
        mw_ref[0, c, d, h] = _bf(kt[i][:, GDN_DV:])
        gm_ref[0, c, idx:idx + 1, :] = jnp.broadcast_to(gam, (1, 128))


def _gdn_ops(qkv_raw, gates, cw8, prm, nck):
    b, s, _ = qkv_raw.shape
    tc = nck * CHUNK
    n = s // CHUNK
    r8 = tc // 8
    nb8 = s // 8
    op_bf16 = jax.ShapeDtypeStruct((b, n, 2, GDN_HEADS, CHUNK, 128), BF16)
    op_spec = pl.BlockSpec((1, nck, 2, GDN_HEADS, CHUNK, 128), lambda bi, si: (bi, si, 0, 0, 0, 0))
    return pl.pallas_call(
        functools.partial(_gdn_kernel, nck=nck),
        out_shape=(op_bf16, op_bf16, op_bf16, op_bf16,
                   jax.ShapeDtypeStruct((b, n, 8, 128), F32)),
        grid=(b, s // tc),
        in_specs=[pl.BlockSpec((1, 8, GDN_CH), lambda bi, si: (bi, jnp.maximum(si * r8 - 1, 0), 0)),
                  pl.BlockSpec((1, tc, GDN_CH), lambda bi, si: (bi, si, 0)),
                  pl.BlockSpec((1, 8, GDN_CH), lambda bi, si: (bi, jnp.minimum((si + 1) * r8, nb8 - 1), 0)),
                  pl.BlockSpec((1, tc, GATE_PAD), lambda bi, si: (bi, si, 0)),
                  pl.BlockSpec((8, GDN_CH), lambda bi, si: (0, 0)),
                  pl.BlockSpec((8, 128), lambda bi, si: (0, 0))],
        out_specs=(op_spec, op_spec, op_spec, op_spec,
                   pl.BlockSpec((1, nck, 8, 128), lambda bi, si: (bi, si, 0, 0))),
        scratch_shapes=[pltpu.VMEM((tc + 16, GDN_CH), F32)],
        compiler_params=_cparams(("parallel", "parallel")),
        name="gdn_ops",
    )(qkv_raw, qkv_raw, qkv_raw, gates, cw8, prm)


def _scan_kernel(mwf, cmf, qtf, olf, gmf, mwb, cmb, qtb, olb, gmb, of_ref, ob_ref, s_ref, *, ks):
    @pl.when(pl.program_id(1) == 0)
    def _():
        s_ref[...] = jnp.zeros_like(s_ref)

    hh = GDN_HEADS
    for k in range(ks):
        for d, (mw, cm, qt, ol, gm, o_ref, kk) in enumerate(((mwf, cmf, qtf, olf, gmf, of_ref, k),
                                                             (mwb, cmb, qtb, olb, gmb, ob_ref, ks - 1 - k))):
            for h in range(hh):
                idx = d * hh + h
                st = s_ref[idx]
                sb = st.astype(BF16)
                o_ref[0, kk * CHUNK:(kk + 1) * CHUNK, h * GDN_DV:(h + 1) * GDN_DV] = (
                    _dot(qt[0, kk, 0, h], sb) + ol[0, kk, 0, h])
                gam = gm[0, kk, idx:idx + 1, :]
                s_ref[idx] = st * gam - _dot(mw[0, kk, 0, h], sb) + cm[0, kk, 0, h]


def _gdn_scan(mw, cm, qt, ol, gm, ks):
    b, n = mw.shape[0], mw.shape[1]
    s = n * CHUNK
    nb = n // ks
    blk = (1, ks, 1, GDN_HEADS, CHUNK, 128)
    fwd = lambda bi, ni: (bi, ni, 0, 0, 0, 0)
    bwd = lambda bi, ni: (bi, nb - 1 - ni, 1, 0, 0, 0)
    gfwd = lambda bi, ni: (bi, ni, 0, 0)
    gbwd = lambda bi, ni: (bi, nb - 1 - ni, 0, 0)
    o_shape = jax.ShapeDtypeStruct((b, s, GDN_W), F32)
    return pl.pallas_call(
        functools.partial(_scan_kernel, ks=ks),
        out_shape=(o_shape, o_shape),
        grid=(b, nb),
        in_specs=[pl.BlockSpec(blk, fwd)] * 4 + [pl.BlockSpec((1, ks, 8, 128), gfwd)]
        + [pl.BlockSpec(blk, bwd)] * 4 + [pl.BlockSpec((1, ks, 8, 128), gbwd)],
        out_specs=(pl.BlockSpec((1, ks * CHUNK, GDN_W), lambda bi, ni: (bi, ni, 0)),
                   pl.BlockSpec((1, ks * CHUNK, GDN_W), lambda bi, ni: (bi, nb - 1 - ni, 0))),
        scratch_shapes=[pltpu.VMEM((2 * GDN_HEADS, GDN_DK, GDN_DV), F32)],
        compiler_params=_cparams(("parallel", "arbitrary")),
        name="gdn_scan",
    )(mw, cm, qt, ol, gm, mw, cm, qt, ol, gm)


def _post_kernel(x_ref, mod_ref, modf_ref, om_ref, of_ref, ob_ref, z_ref, ggdn_ref, wout_ref,
                 gmlp_ref, w1_ref, w2_ref, gfin_ref, y_ref, *, ff_blk):
    d = D_MODEL
    x = x_ref[0]
    mod = mod_ref[0]
    gate_a = mod[:, 2 * d:3 * d]
    shift_m = mod[:, 3 * d:4 * d]
    scale_m = mod[:, 4 * d:5 * d]
    gate_m = mod[:, 5 * d:6 * d]
    modf = modf_ref[0]
    shift_f = modf[:, 0:d]
    scale_f = modf[:, d:2 * d]

    o = of_ref[0] + ob_ref[0]
    z = z_ref[0]
    mixed = _dot(om_ref[0], wout_ref[0:MLA_W, :])
    for h in range(GDN_HEADS):
        sl = slice(h * GDN_DV, (h + 1) * GDN_DV)
        og = _rms(o[:, sl], ggdn_ref[...]) * _silu(z[:, sl])
        mixed = mixed + _dot(og.astype(BF16), wout_ref[MLA_W + h * GDN_DV:MLA_W + (h + 1) * GDN_DV, :])
    x1 = x + gate_a * mixed
    hb = (_rms(x1, gmlp_ref[...]) * (1.0 + scale_m) + shift_m).astype(BF16)
    acc = jnp.zeros_like(x1)
    for j in range(D_FF // ff_blk):
        hid = _dot(hb, w1_ref[:, j * ff_blk:(j + 1) * ff_blk])
        hid = jnp.square(jnp.maximum(hid, 0.0)).astype(BF16)
        acc = acc + _dot(hid, w2_ref[j * ff_blk:(j + 1) * ff_blk, :])
    x2 = x1 + gate_m * acc
    y_ref[0] = _rms(x2, gfin_ref[...]) * (1.0 + scale_f) + shift_f


def _post(x, mod, modf, om, of, ob, z, ggdn, wout, gmlp, w1, w2, gfin, ts):
    b, s, d = x.shape
    const = lambda bi, si: (0, 0)
    tok = lambda w: pl.BlockSpec((1, ts, w), lambda bi, si: (bi, si, 0))
    one = pl.Buffered(1)
    return pl.pallas_call(
        functools.partial(_post_kernel, ff_blk=1024),
        out_shape=jax.ShapeDtypeStruct((b, s, d), F32),
        grid=(b, s // ts),
        in_specs=[tok(d),
                  pl.BlockSpec((1, 1, N_MOD * d), lambda bi, si: (bi, 0, 0)),
                  pl.BlockSpec((1, 1, 2 * d), lambda bi, si: (bi, 0, 0)),
                  tok(MLA_W), tok(GDN_W), tok(GDN_W), tok(GDN_W),
                  pl.BlockSpec((1, GDN_DV), const),
                  pl.BlockSpec((MLA_W + GDN_W, d), const, pipeline_mode=one),
                  pl.BlockSpec((1, d), const),
                  pl.BlockSpec((d, D_FF), const, pipeline_mode=one),
                  pl.BlockSpec((D_FF, d), const, pipeline_mode=one),
                  pl.BlockSpec((1, d), const)],
        out_specs=tok(d),
        compiler_params=_cparams(("parallel", "parallel")),
        name="post",
    )(x, mod, modf, om, of, ob, z, ggdn, wout, gmlp, w1, w2, gfin)


def _swap_halves(t):
    half = t.shape[-1] // 2
    return jnp.concatenate([t[..., half:], t[..., :half]], axis=-1)


def _rope_tables(s):
    inv = 1.0 / (ROPE_THETA ** (jnp.arange(0, QK_ROPE, 2, dtype=F32) / QK_ROPE))
    ang = jnp.arange(s, dtype=F32)[:, None] * inv[None, :]
    cos, sin = jnp.cos(ang), jnp.sin(ang)
    zeros = jnp.zeros((s, 128 - QK_ROPE), F32)
    return (jnp.concatenate([cos, cos, zeros], axis=1),
            jnp.concatenate([-sin, sin, zeros], axis=1))


def _pick(n, pref):
    t = min(n, pref)
    while n % t:
        t //= 2
    return t


def _trunk(x, mod, modf, w):
    b, s, d = x.shape
    cs, sn = _rope_tables(s)
    ts = _pick(s, 512)
    q, k, v, qkv_raw, z, gates = _pre(x, mod, w["gmix"], w["wa"], w["gq"], w["wuq"], w["gkv"], w["wukv"],
                                      cs, sn, ts)
    o_mla = _attention(q, k, v, _pick(s, 1024), _pick(s // 2, 1024))
    nck = 2 if s % (2 * CHUNK) == 0 else 1
    mw, cm, qt, ol, gm = _gdn_ops(qkv_raw, gates, w["cw8"], w["prm"], nck)
    o_f, o_b = _gdn_scan(mw, cm, qt, ol, gm, _pick(s // CHUNK, 8))
    return _post(x, mod, modf, o_mla, o_f, o_b, z, w["ggdn"], w["wout"], w["gmlp"], w["w1"], w["w2"],
                 w["gfin"], ts)


def kernel(x_prompt, x_sample, c_prompt, c_sample, w_ada, b_ada, g_mix, w_in, g_q, w_uq, g_kv, w_ukv,
           conv_w, a_log_f, a_log_b, dt_f, dt_b, g_gdn, w_out, g_mlp, w_mlp_in, w_mlp_out,
           w_ada_f, b_ada_f, g_final):
    d = D_MODEL
    bp, bs = x_prompt.shape[0], x_sample.shape[0]
    c8 = jnp.zeros((8, d), F32).at[:bp].set(c_prompt).at[bp:bp + bs].set(c_sample)
    mod = _ada(c8, w_ada[0], b_ada[0][None, :])[:, None, :]
    modf = _ada(c8, w_ada_f, b_ada_f[None, :])[:, None, :]

    wi = w_in[0]
    o_kr = Q_LORA + KV_LORA
    o_qkv = o_kr + QK_ROPE
    o_z = o_qkv + GDN_CH
    o_g = o_z + GDN_W
    kr = wi[:, o_kr:o_qkv]
    wa = jnp.concatenate([wi[:, :o_kr], kr, _swap_halves(kr), wi[:, o_qkv:o_g], wi[:, o_g:],
                          jnp.zeros((d, GATE_PAD - 4 * GDN_HEADS), F32)], axis=1).astype(BF16)
    wq = w_uq[0].reshape(Q_LORA, MLA_HEADS, QK_NOPE + QK_ROPE)
    wq_r = wq[..., QK_NOPE:]
    wuq = jnp.concatenate([wq[..., :QK_NOPE], wq_r, _swap_halves(wq_r)], axis=-1)
    wuq = wuq.reshape(Q_LORA, MLA_HEADS * QK_PAD).astype(BF16)
    lanes = jnp.zeros((128,), F32)
    prm = jnp.zeros((8, 128), F32)
    prm = prm.at[0].set(lanes.at[0:4].set(a_log_f[0]).at[4:8].set(a_log_b[0]))
    prm = prm.at[1].set(lanes.at[0:4].set(dt_f[0]).at[4:8].set(dt_b[0]))
    w = dict(
        gmix=g_mix, wa=wa, gq=g_q, wuq=wuq, gkv=g_kv, wukv=w_ukv[0].astype(BF16),
        cw8=jnp.zeros((8, GDN_CH), F32).at[:CONV_W].set(conv_w[0]), prm=prm,
        ggdn=g_gdn, wout=w_out[0].astype(BF16), gmlp=g_mlp,
        w1=w_mlp_in[0].astype(BF16), w2=w_mlp_out[0].astype(BF16), gfin=g_final[None, :],
    )
    y_p = _trunk(x_prompt, mod[:bp], modf[:bp], w)
    y_s = _trunk(x_sample, mod[bp:bp + bs], modf[bp:bp + bs], w)
    return (y_p, y_s)
```
